```python
import math
import jax, jax.numpy as jnp
from jax import lax
import numpy as np

D_MODEL = 1024
BATCH = 16
SEQ = 2048
DEPTH = 2

CHUNK = 64
Q_BLOCK = 128
HEAD_DIM = 64
N_HEADS_FOX = D_MODEL // (4 * HEAD_DIM)
N_HEADS_DIFF = D_MODEL // (4 * 2 * HEAD_DIM)
DIFF_QK_DIM = HEAD_DIM
DIFF_V_DIM = 2 * HEAD_DIM
N_HEADS_CHK = D_MODEL // (4 * HEAD_DIM)
CHUNK_LOOKBACK = 8
MAX_REL_DIST = 128
D_FF = 2816
CONV_WIDTH = 3
RMS_EPS = 1e-6
NEG_INF = -1e30

FOX_W = N_HEADS_FOX * HEAD_DIM
DIFF_QK_W = N_HEADS_DIFF * 2 * DIFF_QK_DIM
DIFF_W = N_HEADS_DIFF * DIFF_V_DIM
CHK_W = N_HEADS_CHK * HEAD_DIM
MIX_WIDTH = FOX_W + DIFF_W + CHK_W
IN_SPLITS = (FOX_W, FOX_W, FOX_W, N_HEADS_FOX, DIFF_QK_W, DIFF_QK_W, DIFF_W, CHK_W, CHK_W, CHK_W)
N_IN = 3 * FOX_W + N_HEADS_FOX + 2 * DIFF_QK_W + DIFF_W + 3 * CHK_W

kernel_name = 'hybrid_fox_diff_chunk_convffn_encoder'


def rms_norm(x, g):
    xf = x.astype(jnp.float32)
    y = xf * lax.rsqrt(jnp.mean(xf * xf, axis=-1, keepdims=True) + RMS_EPS)
    return (y * g.astype(jnp.float32)).astype(x.dtype)


def split_columns(proj):
    outs = []
    start = 0
    for w in IN_SPLITS:
        outs.append(proj[..., start:start + w])
        start += w
    return outs


def alibi_slopes(n_heads):
    return jnp.asarray([2.0 ** (-8.0 * (i + 1) / n_heads) for i in range(n_heads)], jnp.float32)


def forgetting_attention(q, k, v, log_f):
    S, D = q.shape[1], q.shape[3]
    cum = jnp.transpose(jnp.cumsum(log_f, axis=1), (0, 2, 1))
    scale = D ** -0.5
    outs = []
    for blk in range(S // Q_BLOCK):
        q0, q1 = blk * Q_BLOCK, (blk + 1) * Q_BLOCK
        s = jnp.einsum('bqhd,bkhd->bhqk', q[:, q0:q1], k[:, :q1]).astype(jnp.float32) * scale
        s = s + cum[:, :, q0:q1, None] - cum[:, :, None, :q1]
        qpos = jnp.arange(q0, q1)[:, None]
        kpos = jnp.arange(q1)[None, :]
        s = jnp.where(kpos <= qpos, s, NEG_INF)
        p = jax.nn.softmax(s, axis=-1).astype(v.dtype)
        outs.append(jnp.einsum('bhqk,bkhd->bqhd', p, v[:, :q1]))
    return jnp.concatenate(outs, axis=1)


def differential_attention(q, k, v, lam, slopes):
    S, Dqk = q.shape[1], q.shape[4]
    scale = Dqk ** -0.5
    outs = []
    for blk in range(S // Q_BLOCK):
        q0, q1 = blk * Q_BLOCK, (blk + 1) * Q_BLOCK
        s = jnp.einsum('bqhcd,bkhcd->bhcqk', q[:, q0:q1], k[:, :q1]).astype(jnp.float32) * scale
        qpos = jnp.arange(q0, q1)[:, None]
        kpos = jnp.arange(q1)[None, :]
        dist = jnp.abs(qpos - kpos).astype(jnp.float32)
        s = s - slopes[None, :, None, None, None] * dist
        allowed = (kpos // CHUNK) <= (qpos // CHUNK)
        s = jnp.where(allowed, s, NEG_INF)
        p = jax.nn.softmax(s, axis=-1)
        a = (p[:, :, 0] - lam * p[:, :, 1]).astype(v.dtype)
        outs.append(jnp.einsum('bhqk,bkhd->bqhd', a, v[:, :q1]))
    return jnp.concatenate(outs, axis=1)


def chunked_band_attention(q, k, v, rel_table):
    B, S, H, D = q.shape
    nc = S // CHUNK
    band = (CHUNK_LOOKBACK + 1) * CHUNK
    qc = q.reshape(B, nc, CHUNK, H, D)

    def gather_band(t):
        tc = t.reshape(B, nc, CHUNK, H, D)
        tp = jnp.pad(tc, ((0, 0), (CHUNK_LOOKBACK, 0), (0, 0), (0, 0), (0, 0)))
        return jnp.concatenate([tp[:, j:j + nc] for j in range(CHUNK_LOOKBACK + 1)], axis=2)

    kb, vb = gather_band(k), gather_band(v)
    s = jnp.einsum('bcqhd,bckhd->bchqk', qc, kb).astype(jnp.float32) * (D ** -0.5)
    i = jnp.arange(CHUNK)[:, None]
    j = jnp.arange(band)[None, :]
    rel = jnp.clip(i + CHUNK_LOOKBACK * CHUNK - j, -(CHUNK - 1), MAX_REL_DIST)
    bias = rel_table.astype(jnp.float32)[:, rel + CHUNK - 1]
    s = s + bias[None, None]
    key_chunk = jnp.arange(nc)[:, None] - CHUNK_LOOKBACK + (jnp.arange(band) // CHUNK)[None, :]
    valid = key_chunk >= 0
    s = jnp.where(valid[None, :, None, None, :], s, NEG_INF)
    p = jax.nn.softmax(s, axis=-1).astype(v.dtype)
    out = jnp.einsum('bchqk,bckhd->bcqhd', p, vb)
    return out.reshape(B, S, H, D)


def conv_gated_mlp(h, w_up, conv_w, conv_b, w_down):
    uv = h @ w_up
    S = uv.shape[1]
    padded = jnp.pad(uv, ((0, 0), (CONV_WIDTH - 1, 0), (0, 0)))
    y = conv_b
    for tap in range(CONV_WIDTH):
        y = y + conv_w[tap] * padded[:, tap:tap + S]
    u, g = jnp.split(y, 2, axis=-1)
    return (jax.nn.silu(g) * u) @ w_down


def setup_inputs(seed: int = 0) -> dict:
    key = jax.random.key(seed)
    ks = jax.random.split(key, 16)

    def nrm(k, shape, scale):
        return jax.random.normal(k, shape, jnp.float32) * scale

    return {
        'x': nrm(ks[0], (BATCH, SEQ, D_MODEL), 1.0),
        'g_mix': 1.0 + nrm(ks[1], (DEPTH, D_MODEL), 0.02),
        'w_in': nrm(ks[2], (DEPTH, D_MODEL, N_IN), D_MODEL ** -0.5),
        'b_fox_f': 2.0 + nrm(ks[3], (DEPTH, N_HEADS_FOX), 0.5),
        'diff_lambda': nrm(ks[4], (DEPTH, 4, DIFF_QK_DIM), 0.1),
        'g_diff': 1.0 + nrm(ks[5], (DEPTH, DIFF_V_DIM), 0.02),
        'rel_bias': nrm(ks[6], (DEPTH, N_HEADS_CHK, CHUNK + MAX_REL_DIST), 0.1),
        'w_out': nrm(ks[7], (DEPTH, MIX_WIDTH, D_MODEL), MIX_WIDTH ** -0.5),
        'g_ffn': 1.0 + nrm(ks[8], (DEPTH, D_MODEL), 0.02),
        'w_ffn_in': nrm(ks[9], (DEPTH, D_MODEL, 2 * D_FF), D_MODEL ** -0.5),
        'conv_w': nrm(ks[10], (DEPTH, CONV_WIDTH, 2 * D_FF), CONV_WIDTH ** -0.5),
        'conv_b': nrm(ks[11], (DEPTH, 2 * D_FF), 0.01),
        'w_ffn_out': nrm(ks[12], (DEPTH, D_FF, D_MODEL), D_FF ** -0.5),
        'g_final': 1.0 + nrm(ks[13], (D_MODEL,), 0.02),
    }


def reference(x, g_mix, w_in, b_fox_f, diff_lambda, g_diff, rel_bias, w_out,
              g_ffn, w_ffn_in, conv_w, conv_b, w_ffn_out, g_final):
    B, S, _ = x.shape
    slopes = alibi_slopes(N_HEADS_DIFF)
    for l in range(DEPTH):
        h = rms_norm(x, g_mix[l])
        proj = h @ w_in[l]
        fq, fk, fv, ff, dq, dk, dv, cq, ck, cv = split_columns(proj)

        log_f = jax.nn.log_sigmoid(ff.astype(jnp.float32) + b_fox_f[l].astype(jnp.float32))
        out_a = forgetting_attention(
            fq.reshape(B, S, N_HEADS_FOX, HEAD_DIM),
            fk.reshape(B, S, N_HEADS_FOX, HEAD_DIM),
            fv.reshape(B, S, N_HEADS_FOX, HEAD_DIM), log_f)

        lambda_init = 0.8 - 0.6 * math.exp(-0.3 * l)
        lam_vecs = diff_lambda[l].astype(jnp.float32)
        lam = (jnp.exp(jnp.sum(lam_vecs[0] * lam_vecs[1]))
               - jnp.exp(jnp.sum(lam_vecs[2] * lam_vecs[3])) + lambda_init)
        out_b = differential_attention(
            dq.reshape(B, S, N_HEADS_DIFF, 2, DIFF_QK_DIM),
            dk.reshape(B, S, N_HEADS_DIFF, 2, DIFF_QK_DIM),
            dv.reshape(B, S, N_HEADS_DIFF, DIFF_V_DIM), lam, slopes)
        out_b = rms_norm(out_b, g_diff[l]) * (1.0 - lambda_init)

        out_c = chunked_band_attention(
            cq.reshape(B, S, N_HEADS_CHK, HEAD_DIM),
            ck.reshape(B, S, N_HEADS_CHK, HEAD_DIM),
            cv.reshape(B, S, N_HEADS_CHK, HEAD_DIM), rel_bias[l])

        mix = jnp.concatenate([out_a.reshape(B, S, FOX_W),
                               out_b.reshape(B, S, DIFF_W),
                               out_c.reshape(B, S, CHK_W)], axis=-1)
        x = x + mix @ w_out[l]

        x = x + conv_gated_mlp(rms_norm(x, g_ffn[l]), w_ffn_in[l], conv_w[l], conv_b[l], w_ffn_out[l])
    return rms_norm(x, g_final)
```

```python
import functools
import math

import jax
import jax.numpy as jnp
from jax import lax
from jax.experimental import pallas as pl
from jax.experimental.pallas import tpu as pltpu

F32 = jnp.float32
BF16 = jnp.bfloat16

D_MODEL = 1024
HEAD_DIM = 64
CHUNK = 64
CHUNK_LOOKBACK = 8
MAX_REL_DIST = 128
D_FF = 2816
RMS_EPS = 1e-6
NEG_INF = -1e30
N_HEADS = 4
N_HEADS_DIFF = 2
FOX_W = 256
DIFF_W = 256
CHK_W = 256
LANES = 128
REL_PAD = 256

TM_IN = 512
TM_FFN = 512
FF_CHUNK = 256
TQ = 256
TK = 256
CHK_BAND = CHUNK_LOOKBACK * CHUNK
CHK_WIN = CHK_BAND + TQ
ROLL_W = 1024
VMEM_LIMIT = 56 * 1024 * 1024


def _dot(a, b):
    return jnp.dot(a, b, preferred_element_type=F32)


def _dot_nt(a, b):
    return lax.dot_general(a, b, (((1,), (1,)), ((), ())), preferred_element_type=F32)


def _split3(x):
    hi = x.astype(BF16)
    r1 = x - hi.astype(F32)
    mid = r1.astype(BF16)
    lo = (r1 - mid.astype(F32)).astype(BF16)
    return hi, mid, lo


def _dot3(x, t):
    hi, mid, lo = _split3(x)
    return _dot(hi, t) + _dot(mid, t) + _dot(lo, t)


def _rms(x, g):
    return x * lax.rsqrt(jnp.mean(x * x, axis=-1, keepdims=True) + RMS_EPS) * g


def _const_spec(shape):
    nd = len(shape)
    return pl.BlockSpec(shape, lambda *_: (0,) * nd)


def _params(n_grid):
    return pltpu.CompilerParams(
        dimension_semantics=("arbitrary",) * n_grid, vmem_limit_bytes=VMEM_LIMIT)


def _in_proj_kernel(x_ref, g_ref, w_ref, wff_ref, bff_ref,
                    fox_ref, diff_ref, chk_ref, cum_ref, carry_ref):
    tm = x_ref.shape[1]
    h = _rms(x_ref[0], g_ref[...]).astype(BF16)
    n_fox, n_diff = fox_ref.shape[2], diff_ref.shape[2]
    fox_ref[0] = _dot(h, w_ref[:, :n_fox]).astype(BF16)
    diff_ref[0] = _dot(h, w_ref[:, n_fox:n_fox + n_diff]).astype(BF16)
    chk_ref[0] = _dot(h, w_ref[:, n_fox + n_diff:]).astype(BF16)

    ff = _dot_nt(wff_ref[...], h) + bff_ref[...]
    log_f = jax.nn.log_sigmoid(ff)
    row = lax.broadcasted_iota(jnp.int32, (tm, tm), 0)
    col = lax.broadcasted_iota(jnp.int32, (tm, tm), 1)
    tri = (row <= col).astype(BF16)

    @pl.when(pl.program_id(1) == 0)
    def _():
        carry_ref[...] = jnp.zeros_like(carry_ref)

    cum = _dot3(log_f, tri) + carry_ref[:, :1]
    cum_ref[0] = cum
    carry_ref[...] = jnp.broadcast_to(cum[:, tm - 1:tm], carry_ref.shape)


def _in_proj(x, g, w_main, wff_t, bff):
    b, s, d = x.shape
    n_main = w_main.shape[1]
    n_fox, n_diff, n_chk = 3 * FOX_W, 2 * DIFF_W + DIFF_W, 3 * CHK_W
    assert n_main == n_fox + n_diff + n_chk
    grid = (b, s // TM_IN)
    row_spec = lambda n: pl.BlockSpec((1, TM_IN, n), lambda i, j: (i, j, 0))
    return pl.pallas_call(
        _in_proj_kernel,
        grid=grid,
        in_specs=[row_spec(d), _const_spec((1, d)), _const_spec((d, n_main)),
                  _const_spec((8, d)), _const_spec((8, 1))],
        out_specs=[row_spec(n_fox), row_spec(n_diff), row_spec(n_chk),
                   pl.BlockSpec((1, 8, TM_IN), lambda i, j: (i, 0, j))],
        out_shape=[jax.ShapeDtypeStruct((b, s, n_fox), BF16),
                   jax.ShapeDtypeStruct((b, s, n_diff), BF16),
                   jax.ShapeDtypeStruct((b, s, n_chk), BF16),
                   jax.ShapeDtypeStruct((b, 8, s), F32)],
        scratch_shapes=[pltpu.VMEM((8, LANES), F32)],
        compiler_params=_params(2),
        name="in_proj",
    )(x, g, w_main, wff_t, bff)


def _half_masks():
    lane = lax.broadcasted_iota(jnp.int32, (1, LANES), 1)
    return lane < HEAD_DIM, lane >= HEAD_DIM


def _prep_streams(q_ref, v_ref, qs_ref, vx_ref, v_masked):
    q = q_ref[0] * jnp.asarray(HEAD_DIM ** -0.5, BF16)
    v = v_ref[0]
    for st, m in enumerate(_half_masks()):
        qs_ref[st] = jnp.where(m, q, jnp.zeros_like(q))
        vx_ref[st, :, :LANES] = jnp.where(m, v, jnp.zeros_like(v)) if v_masked else v
        vx_ref[st, :, LANES:] = jnp.ones_like(v)


def _row_max(m_ref, st):
    return jnp.max(m_ref[st], axis=-1, keepdims=True)


def _fold_max(s):
    m = s[:, :LANES]
    for j in range(1, s.shape[1] // LANES):
        m = jnp.maximum(m, s[:, j * LANES:(j + 1) * LANES])
    return m


def _fox_kernel(q_ref, k_ref, v_ref, cum_ref, o_ref,
                qs_ref, vx_ref, s_ref, m_ref, acc_ref):
    s_len = q_ref.shape[1]
    nq = s_len // TQ
    pair = pl.program_id(1)
    _prep_streams(q_ref, v_ref, qs_ref, vx_ref, v_masked=True)
    ri = lax.broadcasted_iota(jnp.int32, (TQ, TK), 0)
    ci = lax.broadcasted_iota(jnp.int32, (TQ, TK), 1)
    causal = ci <= ri

    def q_body(qb, carry):
        q0 = pl.multiple_of(qb * TQ, TQ)

        def scores(st, kb):
            k0 = pl.multiple_of(kb * TK, TK)
            s = _dot_nt(qs_ref[st, pl.ds(q0, TQ), :], k_ref[0, pl.ds(k0, TK), :])
            return s - cum_ref[0, 2 * pair + st, pl.ds(kb, 1), :]

        def a_body(kb, c):
            for st in range(2):
                s = scores(st, kb)
                s_ref[st, kb] = s
                m_ref[st] = jnp.maximum(m_ref[st], _fold_max(s))
            return c

        m_ref[...] = jnp.full(m_ref.shape, NEG_INF, F32)
        lax.fori_loop(0, qb, a_body, 0)
        for st in range(2):
            s = jnp.where(causal, scores(st, qb), NEG_INF)
            s_ref[st, qb] = s
            m_ref[st] = jnp.maximum(m_ref[st], _fold_max(s))
            m_ref[st] = jnp.broadcast_to(_row_max(m_ref, st), (TQ, LANES))

        def b_body(kb, c):
            k0 = pl.multiple_of(kb * TK, TK)
            for st in range(2):
                e = jnp.exp(s_ref[st, kb] - jnp.tile(m_ref[st], (1, TK // LANES)))
                acc_ref[st] += _dot(e.astype(BF16), vx_ref[st, pl.ds(k0, TK), :])
            return c

        acc_ref[...] = jnp.zeros_like(acc_ref)
        lax.fori_loop(0, qb + 1, b_body, 0)
        out = (acc_ref[0, :, :LANES] / acc_ref[0, :, LANES:]
               + acc_ref[1, :, :LANES] / acc_ref[1, :, LANES:])
        o_ref[0, pl.ds(q0, TQ), :] = out.astype(BF16)
        return carry

    lax.fori_loop(0, nq, q_body, 0)


def _fox_attention(fox, cum):
    b, s, _ = fox.shape
    nk = s // TK
    cum4 = cum.reshape(b, 8, nk, TK)
    n_pair = FOX_W // LANES
    col = lambda off: pl.BlockSpec((1, s, LANES), lambda i, p: (i, 0, off + p))
    return pl.pallas_call(
        _fox_kernel,
        grid=(b, n_pair),
        in_specs=[col(0), col(n_pair), col(2 * n_pair),
                  pl.BlockSpec((1, 8, nk, TK), lambda i, p: (i, 0, 0, 0))],
        out_specs=col(0),
        out_shape=jax.ShapeDtypeStruct((b, s, FOX_W), BF16),
        scratch_shapes=[pltpu.VMEM((2, s, LANES), BF16),
                        pltpu.VMEM((2, s, 2 * LANES), BF16),
                        pltpu.VMEM((2, nk, TQ, TK), F32),
                        pltpu.VMEM((2, TQ, LANES), F32),
                        pltpu.VMEM((2, TQ, 2 * LANES), F32)],
        compiler_params=_params(2),
        name="fox_attn",
    )(fox, fox, fox, cum4)


def _diff_kernel(q_ref, k_ref, v_ref, slope_ref, lam_ref, linit_ref, g_ref, o_ref,
                 qs_ref, vx_ref, s_ref, m_ref, acc_ref):
    s_len = q_ref.shape[1]
    nq = s_len // TQ
    _prep_streams(q_ref, v_ref, qs_ref, vx_ref, v_masked=False)
    slope = slope_ref[0]
    slope_t = jnp.tile(slope, (1, TK // LANES))
    ri = lax.broadcasted_iota(jnp.int32, (TQ, TK), 0)
    ci = lax.broadcasted_iota(jnp.int32, (TQ, TK), 1)
    allowed = (ci // CHUNK) <= (ri // CHUNK)
    diag_bias = slope_t * jnp.where(ci <= ri, ci, 2 * ri - ci).astype(F32)
    diag_bias = jnp.where(allowed, diag_bias, NEG_INF)
    lane_k = lax.broadcasted_iota(jnp.int32, (1, TK), 1)
    lv = lam_ref[...]
    lam = (jnp.exp(jnp.sum(lv[0:1] * lv[1:2], axis=1, keepdims=True))
           - jnp.exp(jnp.sum(lv[2:3] * lv[3:4], axis=1, keepdims=True))
           + linit_ref[...])
    post = g_ref[...] * (1.0 - linit_ref[...])

    def q_body(qb, carry):
        q0 = pl.multiple_of(qb * TQ, TQ)

        def scores(st, kb):
            k0 = pl.multiple_of(kb * TK, TK)
            return _dot_nt(qs_ref[st, pl.ds(q0, TQ), :], k_ref[0, pl.ds(k0, TK), :])

        def a_body(kb, c):
            col_bias = slope_t * (lane_k + (kb - qb) * TK).astype(F32)
            for st in range(2):
                s = scores(st, kb) + col_bias
                s_ref[st, kb] = s
                m_ref[st] = jnp.maximum(m_ref[st], _fold_max(s))
            return c

        m_ref[...] = jnp.full(m_ref.shape, NEG_INF, F32)
        lax.fori_loop(0, qb, a_body, 0)
        for st in range(2):
            s = scores(st, qb) + diag_bias
            s_ref[st, qb] = s
            m_ref[st] = jnp.maximum(m_ref[st], _fold_max(s))
            m_ref[st] = jnp.broadcast_to(_row_max(m_ref, st), (TQ, LANES))

        def b_body(kb, c):
            k0 = pl.multiple_of(kb * TK, TK)
            for st in range(2):
                e = jnp.exp(s_ref[st, kb] - jnp.tile(m_ref[st], (1, TK // LANES)))
                acc_ref[st] += _dot(e.astype(BF16), vx_ref[st, pl.ds(k0, TK), :])
            return c

        acc_ref[...] = jnp.zeros_like(acc_ref)
        lax.fori_loop(0, qb + 1, b_body, 0)
        o = (acc_ref[0, :, :LANES] / acc_ref[0, :, LANES:]
             - lam * (acc_ref[1, :, :LANES] / acc_ref[1, :, LANES:]))
        o = o * lax.rsqrt(jnp.mean(o * o, axis=-1, keepdims=True) + RMS_EPS) * post
        o_ref[0, pl.ds(q0, TQ), :] = o.astype(BF16)
        return carry

    lax.fori_loop(0, nq, q_body, 0)


def _diff_attention(diff, slopes, lam_vecs, linit, g_diff):
    b, s, _ = diff.shape
    nk = s // TK
    col = lambda off: pl.BlockSpec((1, s, LANES), lambda i, h: (i, 0, off + h))
    return pl.pallas_call(
        _diff_kernel,
        grid=(b, N_HEADS_DIFF),
        in_specs=[col(0), col(N_HEADS_DIFF), col(2 * N_HEADS_DIFF),
                  pl.BlockSpec((1, 1, LANES), lambda i, h: (h, 0, 0)),
                  _const_spec((4, HEAD_DIM)), _const_spec((1, LANES)),
                  _const_spec((1, LANES))],
        out_specs=col(0),
        out_shape=jax.ShapeDtypeStruct((b, s, DIFF_W), BF16),
        scratch_shapes=[pltpu.VMEM((2, s, LANES), BF16),
                        pltpu.VMEM((2, s, 2 * LANES), BF16),
                        pltpu.VMEM((2, nk, TQ, TK), F32),
                        pltpu.VMEM((2, TQ, LANES), F32),
                        pltpu.VMEM((2, TQ, 2 * LANES), F32)],
        compiler_params=_params(2),
        name="diff_attn",
    )(diff, diff, diff, slopes, lam_vecs, linit, g_diff)


def _chk_kernel(q_ref, k_ref, v_ref, rel_ref, o_ref,
                qs_ref, kp_ref, vx_ref, bias_ref):
    s_len = q_ref.shape[1]
    nq = s_len // TQ
    pair = pl.program_id(1)

    @pl.when((pl.program_id(0) == 0) & (pair == 0))
    def _():
        u = lax.broadcasted_iota(jnp.int32, (REL_PAD, ROLL_W), 1)
        r = lax.broadcasted_iota(jnp.int32, (REL_PAD, ROLL_W), 0)
        u = jnp.where(u < CHK_WIN, u, u - ROLL_W)
        idx = jnp.clip(CHK_BAND - u, -(CHUNK - 1), MAX_REL_DIST) + (CHUNK - 1)
        onehot = (r == idx).astype(BF16)
        g_all = _dot3(rel_ref[...], onehot)
        ri = lax.broadcasted_iota(jnp.int32, (TQ, CHK_WIN), 0) // CHUNK
        ci = lax.broadcasted_iota(jnp.int32, (TQ, CHK_WIN), 1) // CHUNK
        band = (ci >= ri) & (ci <= ri + CHUNK_LOOKBACK)
        for h in range(N_HEADS):
            g = jnp.broadcast_to(g_all[h:h + 1], (TQ, ROLL_W))
            toep = pltpu.roll(g, 0, 1, stride=1, stride_axis=0)
            bias_ref[h] = jnp.where(band, toep[:, :CHK_WIN], NEG_INF)

    kp_ref[:CHK_BAND] = jnp.zeros((CHK_BAND, LANES), BF16)
    kp_ref[CHK_BAND:] = k_ref[0]
    q = q_ref[0] * jnp.asarray(HEAD_DIM ** -0.5, BF16)
    v = v_ref[0]
    for st, m in enumerate(_half_masks()):
        qs_ref[st] = jnp.where(m, q, jnp.zeros_like(q))
        vx_ref[st, :CHK_BAND, :LANES] = jnp.zeros((CHK_BAND, LANES), BF16)
        vx_ref[st, CHK_BAND:, :LANES] = jnp.where(m, v, jnp.zeros_like(v))
        vx_ref[st, :, LANES:] = jnp.ones((s_len + CHK_BAND, LANES), BF16)
    lane_w = lax.broadcasted_iota(jnp.int32, (1, CHK_WIN), 1)

    def q_body(qb, carry):
        q0 = pl.multiple_of(qb * TQ, TQ)
        exists = (lane_w + q0) >= CHK_BAND
        kwin = kp_ref[pl.ds(q0, CHK_WIN), :]
        out = jnp.zeros((TQ, LANES), F32)
        for st in range(2):
            s = _dot_nt(qs_ref[st, pl.ds(q0, TQ), :], kwin) + bias_ref[2 * pair + st]
            s = jnp.where(exists, s, NEG_INF)
            e = jnp.exp(s - jnp.max(s, axis=-1, keepdims=True))
            r = _dot(e.astype(BF16), vx_ref[st, pl.ds(q0, CHK_WIN), :])
            out = out + r[:, :LANES] / r[:, LANES:]
        o_ref[0, pl.ds(q0, TQ), :] = out.astype(BF16)
        return carry

    lax.fori_loop(0, nq, q_body, 0)


def _chk_attention(chk, rel_pad):
    b, s, _ = chk.shape
    n_pair = CHK_W // LANES
    col = lambda off: pl.BlockSpec((1, s, LANES), lambda i, p: (i, 0, off + p))
    return pl.pallas_call(
        _chk_kernel,
        grid=(b, n_pair),
        in_specs=[col(0), col(n_pair), col(2 * n_pair), _const_spec((8, REL_PAD))],
        out_specs=col(0),
        out_shape=jax.ShapeDtypeStruct((b, s, CHK_W), BF16),
        scratch_shapes=[pltpu.VMEM((2, s, LANES), BF16),
                        pltpu.VMEM((s + CHK_BAND, LANES), BF16),
                        pltpu.VMEM((2, s + CHK_BAND, 2 * LANES), BF16),
                        pltpu.VMEM((N_HEADS, TQ, CHK_WIN), F32)],
        compiler_params=_params(2),
        name="chunk_attn",
    )(chk, chk, chk, rel_pad)


def _ffn_kernel(x_ref, fox_ref, diff_ref, chk_ref, wo_ref, g_ref, wup_ref, cw_ref, cb_ref,
                wdn_ref, gfin_ref, o_ref, uv_ref, carry_ref, *, final_norm):
    tm = x_ref.shape[1]
    mix = jnp.concatenate([fox_ref[0], diff_ref[0], chk_ref[0]], axis=1)
    x1 = x_ref[0] + _dot(mix, wo_ref[...])
    h = _rms(x1, g_ref[...]).astype(BF16)

    @pl.when(pl.program_id(1) == 0)
    def _():
        carry_ref[...] = jnp.zeros_like(carry_ref)

    acc = x1
    for c in range(D_FF // FF_CHUNK):
        ys = []
        for part in range(2):
            c0 = part * D_FF + c * FF_CHUNK
            cols = slice(c0, c0 + FF_CHUNK)
            uv_ref[part, :8] = carry_ref[:, cols]
            uv_ref[part, 8:] = _dot(h, wup_ref[:, cols])
            carry_ref[:, cols] = uv_ref[part, tm:]
            ys.append(cb_ref[:, cols]
                      + cw_ref[2:3, cols] * uv_ref[part, 8:]
                      + cw_ref[1:2, cols] * uv_ref[part, 7:7 + tm]
                      + cw_ref[0:1, cols] * uv_ref[part, 6:6 + tm])
        act = (jax.nn.silu(ys[1]) * ys[0]).astype(BF16)
        acc = acc + _dot(act, wdn_ref[c * FF_CHUNK:(c + 1) * FF_CHUNK, :])
    if final_norm:
        acc = _rms(acc, gfin_ref[...])
    o_ref[0] = acc


def _out_ffn(x, fox_o, diff_o, chk_o, w_out, g_ffn, w_up, conv_w, conv_b, w_down, g_final,
             final_norm):
    b, s, d = x.shape
    grid = (b, s // TM_FFN)
    row_spec = lambda n: pl.BlockSpec((1, TM_FFN, n), lambda i, j: (i, j, 0))
    single = lambda shape: pl.BlockSpec(shape, lambda *_: (0,) * len(shape),
                                        pipeline_mode=pl.Buffered(1))
    return pl.pallas_call(
        functools.partial(_ffn_kernel, final_norm=final_norm),
        grid=grid,
        in_specs=[row_spec(d), row_spec(FOX_W), row_spec(DIFF_W), row_spec(CHK_W),
                  single((FOX_W + DIFF_W + CHK_W, d)), _const_spec((1, d)),
                  single((d, 2 * D_FF)),
                  _const_spec((3, 2 * D_FF)), _const_spec((1, 2 * D_FF)),
                  single((D_FF, d)), _const_spec((1, d))],
        out_specs=row_spec(d),
        out_shape=jax.ShapeDtypeStruct((b, s, d), F32),
        scratch_shapes=[pltpu.VMEM((2, TM_FFN + 8, FF_CHUNK), F32),
                        pltpu.VMEM((8, 2 * D_FF), F32)],
        compiler_params=_params(2),
        name="out_ffn",
    )(x, fox_o, diff_o, chk_o, w_out, g_ffn, w_up, conv_w, conv_b, w_down, g_final)


def kernel(x, g_mix, w_in, b_fox_f, diff_lambda, g_diff, rel_bias, w_out,
           g_ffn, w_ffn_in, conv_w, conv_b, w_ffn_out, g_final):
    depth = w_in.shape[0]
    d = x.shape[-1]
    o_ff = 3 * FOX_W
    o_rest = o_ff + N_HEADS
    slopes = jnp.asarray([2.0 ** (-8.0 * (i + 1) / N_HEADS_DIFF)
                          for i in range(N_HEADS_DIFF)], F32)
    slopes = jnp.broadcast_to(slopes[:, None, None], (N_HEADS_DIFF, 1, LANES))
    for l in range(depth):
        w_main = jnp.concatenate([w_in[l, :, :o_ff], w_in[l, :, o_rest:]], axis=1).astype(BF16)
        wff_t = jnp.zeros((8, d), BF16).at[:N_HEADS].set(
            w_in[l, :, o_ff:o_rest].T.astype(BF16))
        bff = jnp.zeros((8, 1), F32).at[:N_HEADS, 0].set(b_fox_f[l].astype(F32))
        fox, diff, chk, cum = _in_proj(x, g_mix[l][None], w_main, wff_t, bff)

        lambda_init = 0.8 - 0.6 * math.exp(-0.3 * l)
        linit = jnp.full((1, LANES), lambda_init, F32)
        rel_pad = jnp.zeros((8, REL_PAD), F32).at[:N_HEADS, :rel_bias.shape[2]].set(rel_bias[l])

        fox_o = _fox_attention(fox, cum)
        diff_o = _diff_attention(diff, slopes, diff_lambda[l].astype(F32), linit,
                                 g_diff[l][None].astype(F32))
        chk_o = _chk_attention(chk, rel_pad)

        x = _out_ffn(x, fox_o, diff_o, chk_o, w_out[l].astype(BF16), g_ffn[l][None],
                     w_ffn_in[l].astype(BF16), conv_w[l], conv_b[l][None],
                     w_ffn_out[l].astype(BF16), g_final[None], final_norm=(l == depth - 1))
    return x
```

```python
import functools
import math

import jax
import jax.numpy as jnp
from jax import lax
from jax.experimental import pallas as pl
from jax.experimental.pallas import tpu as pltpu

F32 = jnp.float32
BF16 = jnp.bfloat16

D_MODEL = 1024
HEAD_DIM = 64
CHUNK = 64
CHUNK_LOOKBACK = 8
MAX_REL_DIST = 128
D_FF = 2816
RMS_EPS = 1e-6
NEG_INF = -1e30
N_HEADS = 4
N_HEADS_DIFF = 2
FOX_W = 256
DIFF_W = 256
CHK_W = 256
LANES = 128
REL_PAD = 256

TM_IN = 512
TM_FFN = 512
FF_CHUNK = 256
TQ = 256
TK = 256
CHK_BAND = CHUNK_LOOKBACK * CHUNK
CHK_WIN = CHK_BAND + TQ
ROLL_W = 1024
VMEM_LIMIT = 56 * 1024 * 1024


def _dot(a, b):
    return jnp.dot(a, b, preferred_element_type=F32)


def _dot_nt(a, b):
    return lax.dot_general(a, b, (((1,), (1,)), ((), ())), preferred_element_type=F32)


def _split3(x):
    hi = x.astype(BF16)
    r1 = x - hi.astype(F32)
    mid = r1.astype(BF16)
    lo = (r1 - mid.astype(F32)).astype(BF16)
    return hi, mid, lo


def _dot3(x, t):
    hi, mid, lo = _split3(x)
    return _dot(hi, t) + _dot(mid, t) + _dot(lo, t)


def _rms(x, g):
    return x * lax.rsqrt(jnp.mean(x * x, axis=-1, keepdims=True) + RMS_EPS) * g


def _const_spec(shape):
    nd = len(shape)
    return pl.BlockSpec(shape, lambda *_: (0,) * nd)


def _params(n_grid):
    return pltpu.CompilerParams(
        dimension_semantics=("arbitrary",) * n_grid, vmem_limit_bytes=VMEM_LIMIT)


def _in_proj_kernel(x_ref, g_ref, w_ref, wff_ref, bff_ref,
                    fox_ref, diff_ref, chk_ref, cum_ref, carry_ref):
    tm = x_ref.shape[1]
    h = _rms(x_ref[0], g_ref[...]).astype(BF16)
    n_fox, n_diff = fox_ref.shape[2], diff_ref.shape[2]
    fox_ref[0] = _dot(h, w_ref[:, :n_fox]).astype(BF16)
    diff_ref[0] = _dot(h, w_ref[:, n_fox:n_fox + n_diff]).astype(BF16)
    chk_ref[0] = _dot(h, w_ref[:, n_fox + n_diff:]).astype(BF16)

    ff = _dot_nt(wff_ref[...], h) + bff_ref[...]
    log_f = jax.nn.log_sigmoid(ff)
    row = lax.broadcasted_iota(jnp.int32, (tm, tm), 0)
    col = lax.broadcasted_iota(jnp.int32, (tm, tm), 1)
    tri = (row <= col).astype(BF16)

    @pl.when(pl.program_id(1) == 0)
    def _():
        carry_ref[...] = jnp.zeros_like(carry_ref)

    cum = _dot3(log_f, tri) + carry_ref[:, :1]
    cum_ref[0] = cum
    carry_ref[...] = jnp.broadcast_to(cum[:, tm - 1:tm], carry_ref.shape)


def _in_proj(x, g, w_main, wff_t, bff):
    b, s, d = x.shape
    n_main = w_main.shape[1]
    n_fox, n_diff, n_chk = 3 * FOX_W, 3 * DIFF_W, 3 * CHK_W
    assert n_main == n_fox + n_diff + n_chk
    grid = (b, s // TM_IN)
    row_spec = lambda n: pl.BlockSpec((1, TM_IN, n), lambda i, j: (i, j, 0))
    return pl.pallas_call(
        _in_proj_kernel,
        grid=grid,
        in_specs=[row_spec(d), _const_spec((1, d)), _const_spec((d, n_main)),
                  _const_spec((8, d)), _const_spec((8, 1))],
        out_specs=[row_spec(n_fox), row_spec(n_diff), row_spec(n_chk),
                   pl.BlockSpec((1, 8, TM_IN), lambda i, j: (i, 0, j))],
        out_shape=[jax.ShapeDtypeStruct((b, s, n_fox), BF16),
                   jax.ShapeDtypeStruct((b, s, n_diff), BF16),
                   jax.ShapeDtypeStruct((b, s, n_chk), BF16),
                   jax.ShapeDtypeStruct((b, 8, s), F32)],
        scratch_shapes=[pltpu.VMEM((8, LANES), F32)],
        compiler_params=_params(2),
        name="in_proj",
    )(x, g, w_main, wff_t, bff)


def _half_masks():
    lane = lax.broadcasted_iota(jnp.int32, (1, LANES), 1)
    return lane < HEAD_DIM, lane >= HEAD_DIM


def _prep_streams(q_ref, v_ref, qs_ref, vx_ref, v_masked):
    q = q_ref[0] * jnp.asarray(HEAD_DIM ** -0.5, BF16)
    v = v_ref[0]
    for st, m in enumerate(_half_masks()):
        qs_ref[st] = jnp.where(m, q, jnp.zeros_like(q))
        vx_ref[st, :, :LANES] = jnp.where(m, v, jnp.zeros_like(v)) if v_masked else v
        vx_ref[st, :, LANES:] = jnp.ones_like(v)


def _fold_max(s):
    m = s[:, :LANES]
    for j in range(1, s.shape[1] // LANES):
        m = jnp.maximum(m, s[:, j * LANES:(j + 1) * LANES])
    return m


def _softmax_pv(qt, k_ref, vx_st_ref, s_st_ref, blocks):
    m = None
    for j, (k0, bias_fn) in enumerate(blocks):
        s = _dot_nt(qt, k_ref[0, k0:k0 + TK, :]) + bias_fn()
        s_st_ref[:, j * TK:(j + 1) * TK] = s
        fm = _fold_max(s)
        m = fm if m is None else jnp.maximum(m, fm)
    mb = jnp.broadcast_to(jnp.max(m, axis=-1, keepdims=True), (TQ, LANES))
    mb = jnp.tile(mb, (1, TK // LANES))
    acc = None
    for j, (k0, _) in enumerate(blocks):
        e = jnp.exp(s_st_ref[:, j * TK:(j + 1) * TK] - mb)
        r = _dot(e.astype(BF16), vx_st_ref[k0:k0 + TK, :])
        acc = r if acc is None else acc + r
    return acc


def _attn_scratch(s):
    return [pltpu.VMEM((2, s, LANES), BF16),
            pltpu.VMEM((2, s, 2 * LANES), BF16),
            pltpu.VMEM((4, TQ, s), F32)]


def _fox_kernel(q_ref, k_ref, v_ref, cum_ref, o_ref, qs_ref, vx_ref, s_ref):
    nq = q_ref.shape[1] // TQ
    pair = pl.program_id(1)
    _prep_streams(q_ref, v_ref, qs_ref, vx_ref, v_masked=True)
    ri = lax.broadcasted_iota(jnp.int32, (TQ, TK), 0)
    ci = lax.broadcasted_iota(jnp.int32, (TQ, TK), 1)
    causal_bias = jnp.where(ci <= ri, 0.0, NEG_INF).astype(F32)
    for qb in range(nq):
        q0 = qb * TQ
        out = None
        for st in range(2):
            neg_cum = -cum_ref[0, pl.ds(2 * pair + st, 1), :]

            def bias_fn(kb, neg_cum=neg_cum, qb=qb):
                b = neg_cum[:, kb * TK:(kb + 1) * TK]
                return b + causal_bias if kb == qb else b

            blocks = [(kb * TK, functools.partial(bias_fn, kb)) for kb in range(qb + 1)]
            r = _softmax_pv(qs_ref[st, q0:q0 + TQ, :], k_ref, vx_ref.at[st],
                            s_ref.at[2 * (qb % 2) + st], blocks)
            o = r[:, :LANES] / r[:, LANES:]
            out = o if out is None else out + o
        o_ref[0, q0:q0 + TQ, :] = out.astype(BF16)


def _fox_attention(fox, cum):
    b, s, _ = fox.shape
    n_pair = FOX_W // LANES
    col = lambda off: pl.BlockSpec((1, s, LANES), lambda i, p: (i, 0, off + p))
    return pl.pallas_call(
        _fox_kernel,
        grid=(b, n_pair),
        in_specs=[col(0), col(n_pair), col(2 * n_pair),
                  pl.BlockSpec((1, 8, s), lambda i, p: (i, 0, 0))],
        out_specs=col(0),
        out_shape=jax.ShapeDtypeStruct((b, s, FOX_W), BF16),
        scratch_shapes=_attn_scratch(s),
        compiler_params=_params(2),
        name="fox_attn",
    )(fox, fox, fox, cum)


def _diff_kernel(q_ref, k_ref, v_ref, slope_ref, lam_ref, linit_ref, g_ref, o_ref,
                 qs_ref, vx_ref, s_ref):
    nq = q_ref.shape[1] // TQ
    _prep_streams(q_ref, v_ref, qs_ref, vx_ref, v_masked=False)
    slope_t = jnp.tile(slope_ref[0], (1, TK // LANES))
    ri = lax.broadcasted_iota(jnp.int32, (TQ, TK), 0)
    ci = lax.broadcasted_iota(jnp.int32, (TQ, TK), 1)
    allowed = (ci // CHUNK) <= (ri // CHUNK)
    diag_bias = slope_t * jnp.where(ci <= ri, ci, 2 * ri - ci).astype(F32)
    diag_bias = jnp.where(allowed, diag_bias, NEG_INF)
    lane_k = lax.broadcasted_iota(jnp.int32, (1, TK), 1)
    lv = lam_ref[...]
    lam = (jnp.exp(jnp.sum(lv[0:1] * lv[1:2], axis=1, keepdims=True))
           - jnp.exp(jnp.sum(lv[2:3] * lv[3:4], axis=1, keepdims=True))
           + linit_ref[...])
    post = g_ref[...] * (1.0 - linit_ref[...])

    def bias_fn(kb, qb):
        if kb == qb:
            return diag_bias
        return slope_t * (lane_k + (kb - qb) * TK).astype(F32)

    for qb in range(nq):
        q0 = qb * TQ
        blocks = [(kb * TK, functools.partial(bias_fn, kb, qb)) for kb in range(qb + 1)]
        r = [_softmax_pv(qs_ref[st, q0:q0 + TQ, :], k_ref, vx_ref.at[st],
                         s_ref.at[2 * (qb % 2) + st], blocks) for st in range(2)]
        o = (r[0][:, :LANES] / r[0][:, LANES:]
             - lam * (r[1][:, :LANES] / r[1][:, LANES:]))
        o = o * lax.rsqrt(jnp.mean(o * o, axis=-1, keepdims=True) + RMS_EPS) * post
        o_ref[0, q0:q0 + TQ, :] = o.astype(BF16)


def _diff_attention(diff, slopes, lam_vecs, linit, g_diff):
    b, s, _ = diff.shape
    col = lambda off: pl.BlockSpec((1, s, LANES), lambda i, h: (i, 0, off + h))
    return pl.pallas_call(
        _diff_kernel,
        grid=(b, N_HEADS_DIFF),
        in_specs=[col(0), col(N_HEADS_DIFF), col(2 * N_HEADS_DIFF),
                  pl.BlockSpec((1, 1, LANES), lambda i, h: (h, 0, 0)),
                  _const_spec((4, HEAD_DIM)), _const_spec((1, LANES)),
                  _const_spec((1, LANES))],
        out_specs=col(0),
        out_shape=jax.ShapeDtypeStruct((b, s, DIFF_W), BF16),
        scratch_shapes=_attn_scratch(s),
        compiler_params=_params(2),
        name="diff_attn",
    )(diff, diff, diff, slopes, lam_vecs, linit, g_diff)


def _chk_kernel(q_ref, k_ref, v_ref, rel_ref, o_ref, qs_ref, vx_ref, s_ref, bias_ref):
    nq = q_ref.shape[1] // TQ
    pair = pl.program_id(1)

    @pl.when((pl.program_id(0) == 0) & (pair == 0))
    def _():
        u = lax.broadcasted_iota(jnp.int32, (REL_PAD, ROLL_W), 1)
        r = lax.broadcasted_iota(jnp.int32, (REL_PAD, ROLL_W), 0)
        u = jnp.where(u < CHK_WIN, u, u - ROLL_W)
        idx = jnp.clip(CHK_BAND - u, -(CHUNK - 1), MAX_REL_DIST) + (CHUNK - 1)
        onehot = (r == idx).astype(BF16)
        g_all = _dot3(rel_ref[...], onehot)
        ri = lax.broadcasted_iota(jnp.int32, (TQ, CHK_WIN), 0) // CHUNK
        ci = lax.broadcasted_iota(jnp.int32, (TQ, CHK_WIN), 1) // CHUNK
        band = (ci >= ri) & (ci <= ri + CHUNK_LOOKBACK)
        for h in range(N_HEADS):
            g = jnp.broadcast_to(g_all[h:h + 1], (TQ, ROLL_W))
            toep = pltpu.roll(g, 0, 1, stride=1, stride_axis=0)
            bias_ref[h] = jnp.where(band, toep[:, :CHK_WIN], NEG_INF)

    _prep_streams(q_ref, v_ref, qs_ref, vx_ref, v_masked=True)
    n_win = CHK_WIN // TK
    for qb in range(nq):
        q0 = qb * TQ
        out = None
        for st in range(2):
            def bias_fn(w, st=st):
                return bias_ref[2 * pair + st, :, w * TK:(w + 1) * TK]

            blocks = [((qb - (n_win - 1) + w) * TK, functools.partial(bias_fn, w))
                      for w in range(n_win) if qb - (n_win - 1) + w >= 0]
            r = _softmax_pv(qs_ref[st, q0:q0 + TQ, :], k_ref, vx_ref.at[st],
                            s_ref.at[2 * (qb % 2) + st], blocks)
            o = r[:, :LANES] / r[:, LANES:]
            out = o if out is None else out + o
        o_ref[0, q0:q0 + TQ, :] = out.astype(BF16)


def _chk_attention(chk, rel_pad):
    b, s, _ = chk.shape
    n_pair = CHK_W // LANES
    col = lambda off: pl.BlockSpec((1, s, LANES), lambda i, p: (i, 0, off + p))
    return pl.pallas_call(
        _chk_kernel,
        grid=(b, n_pair),
        in_specs=[col(0), col(n_pair), col(2 * n_pair), _const_spec((8, REL_PAD))],
        out_specs=col(0),
        out_shape=jax.ShapeDtypeStruct((b, s, CHK_W), BF16),
        scratch_shapes=_attn_scratch(s)[:2] + [
            pltpu.VMEM((4, TQ, CHK_WIN), F32),
            pltpu.VMEM((N_HEADS, TQ, CHK_WIN), F32)],
        compiler_params=_params(2),
        name="chunk_attn",
    )(chk, chk, chk, rel_pad)


def _ffn_kernel(x_ref, fox_ref, diff_ref, chk_ref, wo_ref, g_ref, wup_ref, cw_ref, cb_ref,
                wdn_ref, gfin_ref, o_ref, uv_ref, carry_ref, *, final_norm):
    tm = x_ref.shape[1]
    mix = jnp.concatenate([fox_ref[0], diff_ref[0], chk_ref[0]], axis=1)
    x1 = x_ref[0] + _dot(mix, wo_ref[...])
    h = _rms(x1, g_ref[...]).astype(BF16)

    @pl.when(pl.program_id(1) == 0)
    def _():
        carry_ref[...] = jnp.zeros_like(carry_ref)

    acc = x1
    for c in range(D_FF // FF_CHUNK):
        ys = []
        for part in range(2):
            c0 = part * D_FF + c * FF_CHUNK
            cols = slice(c0, c0 + FF_CHUNK)
            uv_ref[part, :8] = carry_ref[:, cols]
            uv_ref[part, 8:] = _dot(h, wup_ref[:, cols])
            carry_ref[:, cols] = uv_ref[part, tm:]
            ys.append(cb_ref[:, cols]
                      + cw_ref[2:3, cols] * uv_ref[part, 8:]
                      + cw_ref[1:2, cols] * uv_ref[part, 7:7 + tm]
                      + cw_ref[0:1, cols] * uv_ref[part, 6:6 + tm])
        act = (jax.nn.silu(ys[1]) * ys[0]).astype(BF16)
        acc = acc + _dot(act, wdn_ref[c * FF_CHUNK:(c + 1) * FF_CHUNK, :])
    if final_norm:
        acc = _rms(acc, gfin_ref[...])
    o_ref[0] = acc


def _out_ffn(x, fox_o, diff_o, chk_o, w_out, g_ffn, w_up, conv_w, conv_b, w_down, g_final,
             final_norm):
    b, s, d = x.shape
    grid = (b, s // TM_FFN)
    row_spec = lambda n: pl.BlockSpec((1, TM_FFN, n), lambda i, j: (i, j, 0))
    single = lambda shape: pl.BlockSpec(shape, lambda *_: (0,) * len(shape),
                                        pipeline_mode=pl.Buffered(1))
    return pl.pallas_call(
        functools.partial(_ffn_kernel, final_norm=final_norm),
        grid=grid,
        in_specs=[row_spec(d), row_spec(FOX_W), row_spec(DIFF_W), row_spec(CHK_W),
                  single((FOX_W + DIFF_W + CHK_W, d)), _const_spec((1, d)),
                  single((d, 2 * D_FF)),
                  _const_spec((3, 2 * D_FF)), _const_spec((1, 2 * D_FF)),
                  single((D_FF, d)), _const_spec((1, d))],
        out_specs=row_spec(d),
        out_shape=jax.ShapeDtypeStruct((b, s, d), F32),
        scratch_shapes=[pltpu.VMEM((2, TM_FFN + 8, FF_CHUNK), F32),
                        pltpu.VMEM((8, 2 * D_FF), F32)],
        compiler_params=_params(2),
        name="out_ffn",
    )(x, fox_o, diff_o, chk_o, w_out, g_ffn, w_up, conv_w, conv_b, w_down, g_final)


def kernel(x, g_mix, w_in, b_fox_f, diff_lambda, g_diff, rel_bias, w_out,
           g_ffn, w_ffn_in, conv_w, conv_b, w_ffn_out, g_final):
    depth = w_in.shape[0]
    d = x.shape[-1]
    o_ff = 3 * FOX_W
    o_rest = o_ff + N_HEADS
    slopes = jnp.asarray([2.0 ** (-8.0 * (i + 1) / N_HEADS_DIFF)
                          for i in range(N_HEADS_DIFF)], F32)
    slopes = jnp.broadcast_to(slopes[:, None, None], (N_HEADS_DIFF, 1, LANES))
    for l in range(depth):
        w_main = jnp.concatenate([w_in[l, :, :o_ff], w_in[l, :, o_rest:]], axis=1).astype(BF16)
        wff_t = jnp.zeros((8, d), BF16).at[:N_HEADS].set(
            w_in[l, :, o_ff:o_rest].T.astype(BF16))
        bff = jnp.zeros((8, 1), F32).at[:N_HEADS, 0].set(b_fox_f[l].astype(F32))
        fox, diff, chk, cum = _in_proj(x, g_mix[l][None], w_main, wff_t, bff)

        lambda_init = 0.8 - 0.6 * math.exp(-0.3 * l)
        linit = jnp.full((1, LANES), lambda_init, F32)
        rel_pad = jnp.zeros((8, REL_PAD), F32).at[:N_HEADS, :rel_bias.shape[2]].set(rel_bias[l])

        fox_o = _fox_attention(fox, cum)
        diff_o = _diff_attention(diff, slopes, diff_lambda[l].astype(F32), linit,
                                 g_diff[l][None].astype(F32))
        chk_o = _chk_attention(chk, rel_pad)

        x = _out_ffn(x, fox_o, diff_o, chk_o, w_out[l].astype(BF16), g_ffn[l][None],
                     w_ffn_in[l].astype(BF16), conv_w[l], conv_b[l][None],
                     w_ffn_out[l].astype(BF16), g_final[None], final_norm=(l == depth - 1))
    return x
```

```python
import functools
import math

import jax
import jax.numpy as jnp
from jax import lax
from jax.experimental import pallas as pl
from jax.experimental.pallas import tpu as pltpu

F32 = jnp.float32
BF16 = jnp.bfloat16

D_MODEL = 1024
HEAD_DIM = 64
CHUNK = 64
CHUNK_LOOKBACK = 8
MAX_REL_DIST = 128
D_FF = 2816
RMS_EPS = 1e-6
NEG_INF = -1e30
N_HEADS = 4
N_HEADS_DIFF = 2
FOX_W = 256
DIFF_W = 256
CHK_W = 256
LANES = 128
REL_PAD = 256

TM_IN = 512
TM_FFN = 512
FF_CHUNK = 256
TQ = 256
TK = 256
CHK_BAND = CHUNK_LOOKBACK * CHUNK
CHK_WIN = CHK_BAND + TQ
ROLL_W = 1024
VMEM_LIMIT = 56 * 1024 * 1024


def _dot(a, b):
    return jnp.dot(a, b, preferred_element_type=F32)


def _dot_nt(a, b):
    return lax.dot_general(a, b, (((1,), (1,)), ((), ())), preferred_element_type=F32)


def _split3(x):
    hi = x.astype(BF16)
    r1 = x - hi.astype(F32)
    mid = r1.astype(BF16)
    lo = (r1 - mid.astype(F32)).astype(BF16)
    return hi, mid, lo


def _dot3(x, t):
    hi, mid, lo = _split3(x)
    return _dot(hi, t) + _dot(mid, t) + _dot(lo, t)


def _rms(x, g):
    return x * lax.rsqrt(jnp.mean(x * x, axis=-1, keepdims=True) + RMS_EPS) * g


def _const_spec(shape):
    nd = len(shape)
    return pl.BlockSpec(shape, lambda *_: (0,) * nd)


def _params(n_grid):
    return pltpu.CompilerParams(
        dimension_semantics=("arbitrary",) * n_grid, vmem_limit_bytes=VMEM_LIMIT)


def _in_proj_kernel(x_ref, g_ref, w_ref, wff_ref, bff_ref,
                    fox_ref, diff_ref, chk_ref, cum_ref, carry_ref):
    tm = x_ref.shape[1]
    h = _rms(x_ref[0], g_ref[...]).astype(BF16)
    n_fox, n_diff = fox_ref.shape[2], diff_ref.shape[2]
    fox_ref[0] = _dot(h, w_ref[:, :n_fox]).astype(BF16)
    diff_ref[0] = _dot(h, w_ref[:, n_fox:n_fox + n_diff]).astype(BF16)
    chk_ref[0] = _dot(h, w_ref[:, n_fox + n_diff:]).astype(BF16)

    ff = _dot_nt(wff_ref[...], h) + bff_ref[...]
    log_f = jax.nn.log_sigmoid(ff)
    row = lax.broadcasted_iota(jnp.int32, (tm, tm), 0)
    col = lax.broadcasted_iota(jnp.int32, (tm, tm), 1)
    tri = (row <= col).astype(BF16)

    @pl.when(pl.program_id(1) == 0)
    def _():
        carry_ref[...] = jnp.zeros_like(carry_ref)

    cum = _dot3(log_f, tri) + carry_ref[:, :1]
    cum_ref[0] = cum
    carry_ref[...] = jnp.broadcast_to(cum[:, tm - 1:tm], carry_ref.shape)


def _in_proj(x, g, w_main, wff_t, bff):
    b, s, d = x.shape
    n_main = w_main.shape[1]
    n_fox, n_diff, n_chk = 3 * FOX_W, 3 * DIFF_W, 3 * CHK_W
    assert n_main == n_fox + n_diff + n_chk
    grid = (b, s // TM_IN)
    row_spec = lambda n: pl.BlockSpec((1, TM_IN, n), lambda i, j: (i, j, 0))
    return pl.pallas_call(
        _in_proj_kernel,
        grid=grid,
        in_specs=[row_spec(d), _const_spec((1, d)), _const_spec((d, n_main)),
                  _const_spec((8, d)), _const_spec((8, 1))],
        out_specs=[row_spec(n_fox), row_spec(n_diff), row_spec(n_chk),
                   pl.BlockSpec((1, 8, TM_IN), lambda i, j: (i, 0, j))],
        out_shape=[jax.ShapeDtypeStruct((b, s, n_fox), BF16),
                   jax.ShapeDtypeStruct((b, s, n_diff), BF16),
                   jax.ShapeDtypeStruct((b, s, n_chk), BF16),
                   jax.ShapeDtypeStruct((b, 8, s), F32)],
        scratch_shapes=[pltpu.VMEM((8, LANES), F32)],
        compiler_params=_params(2),
        name="in_proj",
    )(x, g, w_main, wff_t, bff)


def _half_masks():
    lane = lax.broadcasted_iota(jnp.int32, (1, LANES), 1)
    return lane < HEAD_DIM, lane >= HEAD_DIM


def _prep_streams(q_ref, v_ref, qs_ref, vx_ref, v_masked):
    q = q_ref[0] * jnp.asarray(HEAD_DIM ** -0.5, BF16)
    v = v_ref[0]
    for st, m in enumerate(_half_masks()):
        qs_ref[st] = jnp.where(m, q, jnp.zeros_like(q))
        vx_ref[st, :, :LANES] = jnp.where(m, v, jnp.zeros_like(v)) if v_masked else v
        vx_ref[st, :, LANES:] = jnp.ones_like(v)


def _fold_max(s):
    m = s[:, :LANES]
    for j in range(1, s.shape[1] // LANES):
        m = jnp.maximum(m, s[:, j * LANES:(j + 1) * LANES])
    return m


def _softmax_pv(qt, k_ref, vx_st_ref, s_st_ref, blocks):
    m = None
    for j, (k0, bias_fn) in enumerate(blocks):
        s = _dot_nt(qt, k_ref[0, k0:k0 + TK, :]) + bias_fn()
        s_st_ref[:, j * TK:(j + 1) * TK] = s
        fm = _fold_max(s)
        m = fm if m is None else jnp.maximum(m, fm)
    mb = jnp.broadcast_to(jnp.max(m, axis=-1, keepdims=True), (TQ, LANES))
    mb = jnp.tile(mb, (1, TK // LANES))
    acc = None
    for j, (k0, _) in enumerate(blocks):
        e = jnp.exp(s_st_ref[:, j * TK:(j + 1) * TK] - mb)
        r = _dot(e.astype(BF16), vx_st_ref[k0:k0 + TK, :])
        acc = r if acc is None else acc + r
    return acc


def _attn_scratch(s):
    return [pltpu.VMEM((2, s, LANES), BF16),
            pltpu.VMEM((2, s, 2 * LANES), BF16),
            pltpu.VMEM((4, TQ, s), F32)]


def _fox_kernel(q_ref, k_ref, v_ref, cum_ref, o_ref, qs_ref, vx_ref, s_ref):
    nq = q_ref.shape[1] // TQ
    pair = pl.program_id(1)
    _prep_streams(q_ref, v_ref, qs_ref, vx_ref, v_masked=True)
    ri = lax.broadcasted_iota(jnp.int32, (TQ, TK), 0)
    ci = lax.broadcasted_iota(jnp.int32, (TQ, TK), 1)
    causal_bias = jnp.where(ci <= ri, 0.0, NEG_INF).astype(F32)
    for qb in range(nq):
        q0 = qb * TQ
        out = None
        for st in range(2):
            neg_cum = -cum_ref[0, pl.ds(2 * pair + st, 1), :]

            def bias_fn(kb, neg_cum=neg_cum, qb=qb):
                b = neg_cum[:, kb * TK:(kb + 1) * TK]
                return b + causal_bias if kb == qb else b

            blocks = [(kb * TK, functools.partial(bias_fn, kb)) for kb in range(qb + 1)]
            r = _softmax_pv(qs_ref[st, q0:q0 + TQ, :], k_ref, vx_ref.at[st],
                            s_ref.at[2 * (qb % 2) + st], blocks)
            o = r[:, :LANES] / r[:, LANES:]
            out = o if out is None else out + o
        o_ref[0, q0:q0 + TQ, :] = out.astype(BF16)


def _fox_attention(fox, cum):
    b, s, _ = fox.shape
    n_pair = FOX_W // LANES
    col = lambda off: pl.BlockSpec((1, s, LANES), lambda i, p: (i, 0, off + p))
    return pl.pallas_call(
        _fox_kernel,
        grid=(b, n_pair),
        in_specs=[col(0), col(n_pair), col(2 * n_pair),
                  pl.BlockSpec((1, 8, s), lambda i, p: (i, 0, 0))],
        out_specs=col(0),
        out_shape=jax.ShapeDtypeStruct((b, s, FOX_W), BF16),
        scratch_shapes=_attn_scratch(s),
        compiler_params=_params(2),
        name="fox_attn",
    )(fox, fox, fox, cum)


def _diff_kernel(q_ref, k_ref, v_ref, slope_ref, lam_ref, linit_ref, g_ref, o_ref,
                 qs_ref, vx_ref, s_ref):
    nq = q_ref.shape[1] // TQ
    _prep_streams(q_ref, v_ref, qs_ref, vx_ref, v_masked=False)
    slope_t = jnp.tile(slope_ref[0], (1, TK // LANES))
    ri = lax.broadcasted_iota(jnp.int32, (TQ, TK), 0)
    ci = lax.broadcasted_iota(jnp.int32, (TQ, TK), 1)
    allowed = (ci // CHUNK) <= (ri // CHUNK)
    diag_bias = slope_t * jnp.where(ci <= ri, ci, 2 * ri - ci).astype(F32)
    diag_bias = jnp.where(allowed, diag_bias, NEG_INF)
    lane_k = lax.broadcasted_iota(jnp.int32, (1, TK), 1)
    lv = lam_ref[...]
    lam = (jnp.exp(jnp.sum(lv[0:1] * lv[1:2], axis=1, keepdims=True))
           - jnp.exp(jnp.sum(lv[2:3] * lv[3:4], axis=1, keepdims=True))
           + linit_ref[...])
    post = g_ref[...] * (1.0 - linit_ref[...])

    def bias_fn(kb, qb):
        if kb == qb:
            return diag_bias
        return slope_t * (lane_k + (kb - qb) * TK).astype(F32)

    for qb in range(nq):
        q0 = qb * TQ
        blocks = [(kb * TK, functools.partial(bias_fn, kb, qb)) for kb in range(qb + 1)]
        r = [_softmax_pv(qs_ref[st, q0:q0 + TQ, :], k_ref, vx_ref.at[st],
                         s_ref.at[2 * (qb % 2) + st], blocks) for st in range(2)]
        o = (r[0][:, :LANES] / r[0][:, LANES:]
             - lam * (r[1][:, :LANES] / r[1][:, LANES:]))
        o = o * lax.rsqrt(jnp.mean(o * o, axis=-1, keepdims=True) + RMS_EPS) * post
        o_ref[0, q0:q0 + TQ, :] = o.astype(BF16)


def _diff_attention(diff, slopes, lam_vecs, linit, g_diff):
    b, s, _ = diff.shape
    col = lambda off: pl.BlockSpec((1, s, LANES), lambda i, h: (i, 0, off + h))
    return pl.pallas_call(
        _diff_kernel,
        grid=(b, N_HEADS_DIFF),
        in_specs=[col(0), col(N_HEADS_DIFF), col(2 * N_HEADS_DIFF),
                  pl.BlockSpec((1, 1, LANES), lambda i, h: (h, 0, 0)),
                  _const_spec((4, HEAD_DIM)), _const_spec((1, LANES)),
                  _const_spec((1, LANES))],
        out_specs=col(0),
        out_shape=jax.ShapeDtypeStruct((b, s, DIFF_W), BF16),
        scratch_shapes=_attn_scratch(s),
        compiler_params=_params(2),
        name="diff_attn",
    )(diff, diff, diff, slopes, lam_vecs, linit, g_diff)


def _chk_kernel(q_ref, k_ref, v_ref, rel_ref, o_ref, qs_ref, vx_ref, s_ref, bias_ref):
    nq = q_ref.shape[1] // TQ
    pair = pl.program_id(1)

    @pl.when((pl.program_id(0) == 0) & (pair == 0))
    def _():
        u = lax.broadcasted_iota(jnp.int32, (REL_PAD, ROLL_W), 1)
        r = lax.broadcasted_iota(jnp.int32, (REL_PAD, ROLL_W), 0)
        u = jnp.where(u < CHK_WIN, u, u - ROLL_W)
        idx = jnp.clip(CHK_BAND - u, -(CHUNK - 1), MAX_REL_DIST) + (CHUNK - 1)
        onehot = (r == idx).astype(BF16)
        g_all = _dot3(rel_ref[...], onehot)
        ri = lax.broadcasted_iota(jnp.int32, (TQ, CHK_WIN), 0) // CHUNK
        ci = lax.broadcasted_iota(jnp.int32, (TQ, CHK_WIN), 1) // CHUNK
        band = (ci >= ri) & (ci <= ri + CHUNK_LOOKBACK)
        for h in range(N_HEADS):
            g = jnp.broadcast_to(g_all[h:h + 1], (TQ, ROLL_W))
            toep = pltpu.roll(g, 0, 1, stride=1, stride_axis=0)
            bias_ref[h] = jnp.where(band, toep[:, :CHK_WIN], NEG_INF)

    _prep_streams(q_ref, v_ref, qs_ref, vx_ref, v_masked=True)
    n_win = CHK_WIN // TK
    for qb in range(nq):
        q0 = qb * TQ
        out = None
        for st in range(2):
            def bias_fn(w, st=st):
                return bias_ref[2 * pair + st, :, w * TK:(w + 1) * TK]

            blocks = [((qb - (n_win - 1) + w) * TK, functools.partial(bias_fn, w))
                      for w in range(n_win) if qb - (n_win - 1) + w >= 0]
            r = _softmax_pv(qs_ref[st, q0:q0 + TQ, :], k_ref, vx_ref.at[st],
                            s_ref.at[2 * (qb % 2) + st], blocks)
            o = r[:, :LANES] / r[:, LANES:]
            out = o if out is None else out + o
        o_ref[0, q0:q0 + TQ, :] = out.astype(BF16)


def _chk_attention(chk, rel_pad):
    b, s, _ = chk.shape
    n_pair = CHK_W // LANES
    col = lambda off: pl.BlockSpec((1, s, LANES), lambda i, p: (i, 0, off + p))
    return pl.pallas_call(
        _chk_kernel,
        grid=(b, n_pair),
        in_specs=[col(0), col(n_pair), col(2 * n_pair), _const_spec((8, REL_PAD))],
        out_specs=col(0),
        out_shape=jax.ShapeDtypeStruct((b, s, CHK_W), BF16),
        scratch_shapes=_attn_scratch(s)[:2] + [
            pltpu.VMEM((4, TQ, CHK_WIN), F32),
            pltpu.VMEM((N_HEADS, TQ, CHK_WIN), F32)],
        compiler_params=_params(2),
        name="chunk_attn",
    )(chk, chk, chk, rel_pad)


def _ffn_kernel(x_ref, fox_ref, diff_ref, chk_ref, wo_ref, g_ref, wup_ref, cw_ref, cb_ref,
                wdn_ref, gfin_ref, o_ref, carry_ref, act_ref, *, final_norm):
    tm = x_ref.shape[1]
    mix = jnp.concatenate([fox_ref[0], diff_ref[0], chk_ref[0]], axis=1)
    x1 = x_ref[0] + _dot(mix, wo_ref[...])
    h = _rms(x1, g_ref[...]).astype(BF16)

    @pl.when(pl.program_id(1) == 0)
    def _():
        carry_ref[...] = jnp.zeros_like(carry_ref)

    sub = lax.broadcasted_iota(jnp.int32, (8, FF_CHUNK), 0)

    def shifted(d, prev, n):
        r = pltpu.roll(d, n, 0)
        head = jnp.where(sub < n, pltpu.roll(prev, n, 0), r[:8])
        return jnp.concatenate([head, r[8:]], axis=0)

    for c in range(D_FF // FF_CHUNK):
        ys = []
        for part in range(2):
            c0 = part * D_FF + c * FF_CHUNK
            cols = slice(c0, c0 + FF_CHUNK)
            d = _dot(h, wup_ref[:, cols])
            prev = carry_ref[:, cols]
            carry_ref[:, cols] = d[tm - 8:]
            ys.append(cb_ref[:, cols]
                      + cw_ref[2:3, cols] * d
                      + cw_ref[1:2, cols] * shifted(d, prev, 1)
                      + cw_ref[0:1, cols] * shifted(d, prev, 2))
        hg = 0.5 * ys[1]
        act_ref[:, c * FF_CHUNK:(c + 1) * FF_CHUNK] = (
            (hg + hg * jnp.tanh(hg)) * ys[0]).astype(BF16)
    acc = x1 + _dot(act_ref[...], wdn_ref[...])
    if final_norm:
        acc = _rms(acc, gfin_ref[...])
    o_ref[0] = acc


def _out_ffn(x, fox_o, diff_o, chk_o, w_out, g_ffn, w_up, conv_w, conv_b, w_down, g_final,
             final_norm):
    b, s, d = x.shape
    grid = (b, s // TM_FFN)
    row_spec = lambda n: pl.BlockSpec((1, TM_FFN, n), lambda i, j: (i, j, 0))
    single = lambda shape: pl.BlockSpec(shape, lambda *_: (0,) * len(shape),
                                        pipeline_mode=pl.Buffered(1))
    return pl.pallas_call(
        functools.partial(_ffn_kernel, final_norm=final_norm),
        grid=grid,
        in_specs=[row_spec(d), row_spec(FOX_W), row_spec(DIFF_W), row_spec(CHK_W),
                  single((FOX_W + DIFF_W + CHK_W, d)), _const_spec((1, d)),
                  single((d, 2 * D_FF)),
                  _const_spec((3, 2 * D_FF)), _const_spec((1, 2 * D_FF)),
                  single((D_FF, d)), _const_spec((1, d))],
        out_specs=row_spec(d),
        out_shape=jax.ShapeDtypeStruct((b, s, d), F32),
        scratch_shapes=[pltpu.VMEM((8, 2 * D_FF), F32),
                        pltpu.VMEM((TM_FFN, D_FF), BF16)],
        compiler_params=_params(2),
        name="out_ffn",
    )(x, fox_o, diff_o, chk_o, w_out, g_ffn, w_up, conv_w, conv_b, w_down, g_final)


def kernel(x, g_mix, w_in, b_fox_f, diff_lambda, g_diff, rel_bias, w_out,
           g_ffn, w_ffn_in, conv_w, conv_b, w_ffn_out, g_final):
    depth = w_in.shape[0]
    d = x.shape[-1]
    o_ff = 3 * FOX_W
    o_rest = o_ff + N_HEADS
    slopes = jnp.asarray([2.0 ** (-8.0 * (i + 1) / N_HEADS_DIFF)
                          for i in range(N_HEADS_DIFF)], F32)
    slopes = jnp.broadcast_to(slopes[:, None, None], (N_HEADS_DIFF, 1, LANES))
    for l in range(depth):
        w_main = jnp.concatenate([w_in[l, :, :o_ff], w_in[l, :, o_rest:]], axis=1).astype(BF16)
        wff_t = jnp.zeros((8, d), BF16).at[:N_HEADS].set(
            w_in[l, :, o_ff:o_rest].T.astype(BF16))
        bff = jnp.zeros((8, 1), F32).at[:N_HEADS, 0].set(b_fox_f[l].astype(F32))
        fox, diff, chk, cum = _in_proj(x, g_mix[l][None], w_main, wff_t, bff)

        lambda_init = 0.8 - 0.6 * math.exp(-0.3 * l)
        linit = jnp.full((1, LANES), lambda_init, F32)
        rel_pad = jnp.zeros((8, REL_PAD), F32).at[:N_HEADS, :rel_bias.shape[2]].set(rel_bias[l])

        fox_o = _fox_attention(fox, cum)
        diff_o = _diff_attention(diff, slopes, diff_lambda[l].astype(F32), linit,
                                 g_diff[l][None].astype(F32))
        chk_o = _chk_attention(chk, rel_pad)

        x = _out_ffn(x, fox_o, diff_o, chk_o, w_out[l].astype(BF16), g_ffn[l][None],
                     w_ffn_in[l].astype(BF16), conv_w[l], conv_b[l][None],
                     w_ffn_out[l].astype(BF16), g_final[None], final_norm=(l == depth - 1))
    return x
```

```python
import functools
import math

import jax
import jax.numpy as jnp
from jax import lax
from jax.experimental import pallas as pl
from jax.experimental.pallas import tpu as pltpu

F32 = jnp.float32
BF16 = jnp.bfloat16

D_MODEL = 1024
HEAD_DIM = 64
CHUNK = 64
CHUNK_LOOKBACK = 8
MAX_REL_DIST = 128
D_FF = 2816
RMS_EPS = 1e-6
NEG_INF = -1e30
N_HEADS = 4
N_HEADS_DIFF = 2
FOX_W = 256
DIFF_W = 256
CHK_W = 256
LANES = 128
REL_PAD = 256

TM_IN = 512
TM_FFN = 512
FF_CHUNK = 256
TQ = 256
TK = 512
CHK_BAND = CHUNK_LOOKBACK * CHUNK
CHK_WIN = CHK_BAND + TQ
ROLL_W = 1024
VMEM_LIMIT = 56 * 1024 * 1024


def _dot(a, b):
    return jnp.dot(a, b, preferred_element_type=F32)


def _dot_nt(a, b):
    return lax.dot_general(a, b, (((1,), (1,)), ((), ())), preferred_element_type=F32)


def _split3(x):
    hi = x.astype(BF16)
    r1 = x - hi.astype(F32)
    mid = r1.astype(BF16)
    lo = (r1 - mid.astype(F32)).astype(BF16)
    return hi, mid, lo


def _dot3(x, t):
    hi, mid, lo = _split3(x)
    return _dot(hi, t) + _dot(mid, t) + _dot(lo, t)


def _rms(x, g):
    return x * lax.rsqrt(jnp.mean(x * x, axis=-1, keepdims=True) + RMS_EPS) * g


def _const_spec(shape):
    nd = len(shape)
    return pl.BlockSpec(shape, lambda *_: (0,) * nd)


def _params(n_grid):
    return pltpu.CompilerParams(
        dimension_semantics=("arbitrary",) * n_grid, vmem_limit_bytes=VMEM_LIMIT)


def _in_proj_kernel(x_ref, g_ref, w_ref, wff_ref, bff_ref,
                    fox_ref, diff_ref, chk_ref, cum_ref, carry_ref):
    tm = x_ref.shape[1]
    h = _rms(x_ref[0], g_ref[...]).astype(BF16)

    ff = _dot_nt(wff_ref[...], h) + bff_ref[...]
    log_f = jax.nn.log_sigmoid(ff)
    row = lax.broadcasted_iota(jnp.int32, (tm, tm), 0)
    col = lax.broadcasted_iota(jnp.int32, (tm, tm), 1)
    tri = (row <= col).astype(BF16)

    @pl.when(pl.program_id(1) == 0)
    def _():
        carry_ref[...] = jnp.zeros_like(carry_ref)

    cum = _dot3(log_f, tri) + carry_ref[:, :1]
    cum_ref[0] = cum
    carry_ref[...] = jnp.broadcast_to(cum[:, tm - 1:tm], carry_ref.shape)

    n_fox, n_diff = fox_ref.shape[2], diff_ref.shape[2]
    fox_ref[0] = _dot(h, w_ref[:, :n_fox]).astype(BF16)
    diff_ref[0] = _dot(h, w_ref[:, n_fox:n_fox + n_diff]).astype(BF16)
    chk_ref[0] = _dot(h, w_ref[:, n_fox + n_diff:]).astype(BF16)


def _in_proj(x, g, w_main, wff_t, bff):
    b, s, d = x.shape
    n_main = w_main.shape[1]
    n_fox, n_diff, n_chk = 3 * FOX_W, 3 * DIFF_W, 3 * CHK_W
    assert n_main == n_fox + n_diff + n_chk
    grid = (b, s // TM_IN)
    row_spec = lambda n: pl.BlockSpec((1, TM_IN, n), lambda i, j: (i, j, 0))
    return pl.pallas_call(
        _in_proj_kernel,
        grid=grid,
        in_specs=[row_spec(d), _const_spec((1, d)), _const_spec((d, n_main)),
                  _const_spec((8, d)), _const_spec((8, 1))],
        out_specs=[row_spec(n_fox), row_spec(n_diff), row_spec(n_chk),
                   pl.BlockSpec((1, 8, TM_IN), lambda i, j: (i, 0, j))],
        out_shape=[jax.ShapeDtypeStruct((b, s, n_fox), BF16),
                   jax.ShapeDtypeStruct((b, s, n_diff), BF16),
                   jax.ShapeDtypeStruct((b, s, n_chk), BF16),
                   jax.ShapeDtypeStruct((b, 8, s), F32)],
        scratch_shapes=[pltpu.VMEM((8, LANES), F32)],
        compiler_params=_params(2),
        name="in_proj",
    )(x, g, w_main, wff_t, bff)


def _half_masks():
    lane = lax.broadcasted_iota(jnp.int32, (1, LANES), 1)
    return lane < HEAD_DIM, lane >= HEAD_DIM


def _prep_streams(q_ref, v_ref, qs_ref, vx_ref, v_masked):
    q = q_ref[0] * jnp.asarray(HEAD_DIM ** -0.5, BF16)
    v = v_ref[0]
    for st, m in enumerate(_half_masks()):
        qs_ref[st] = jnp.where(m, q, jnp.zeros_like(q))
        vx_ref[st, :, :LANES] = jnp.where(m, v, jnp.zeros_like(v)) if v_masked else v
        vx_ref[st, :, LANES:] = jnp.ones_like(v)


def _fold_max(s):
    m = s[:, :LANES]
    for j in range(1, s.shape[1] // LANES):
        m = jnp.maximum(m, s[:, j * LANES:(j + 1) * LANES])
    return m


def _key_blocks(k_lo, k_hi, width):
    return [(k0, min(width, k_hi - k0)) for k0 in range(k_lo, k_hi, width)]


class _QueryTile:
    def __init__(self, qb, blocks, bias_fn):
        self.qb = qb
        self.q0 = qb * TQ
        self.blocks = blocks
        self.offs = [sum(n for _, n in blocks[:j]) for j in range(len(blocks))]
        self.bias_fn = bias_fn
        self.m = [None, None]
        self.mb = [None, None]
        self.acc = [None, None]


def _attend(tiles, qs_ref, k_ref, vx_ref, s_ref, finish):
    def s_slot(t, st):
        return s_ref.at[2 * (t.qb % 2) + st]

    def pass_a(t, j):
        k0, n = t.blocks[j]
        kt = k_ref[0, k0:k0 + n, :]
        for st in range(2):
            s = _dot_nt(qs_ref[st, t.q0:t.q0 + TQ, :], kt) + t.bias_fn(st, k0, n)
            s_slot(t, st)[:, t.offs[j]:t.offs[j] + n] = s
            fm = _fold_max(s)
            t.m[st] = fm if t.m[st] is None else jnp.maximum(t.m[st], fm)

    def end_a(t):
        for st in range(2):
            t.mb[st] = jnp.broadcast_to(
                jnp.max(t.m[st], axis=-1, keepdims=True), (TQ, LANES))

    def pass_b(t, j):
        k0, n = t.blocks[j]
        for st in range(2):
            s = s_slot(t, st)[:, t.offs[j]:t.offs[j] + n]
            e = jnp.exp(s - jnp.tile(t.mb[st], (1, n // LANES)))
            r = _dot(e.astype(BF16), vx_ref[st, k0:k0 + n, :])
            t.acc[st] = r if t.acc[st] is None else t.acc[st] + r

    for j in range(len(tiles[0].blocks)):
        pass_a(tiles[0], j)
    end_a(tiles[0])
    for i, t in enumerate(tiles):
        nxt = tiles[i + 1] if i + 1 < len(tiles) else None
        n_a = len(nxt.blocks) if nxt else 0
        for j in range(max(n_a, len(t.blocks))):
            if j < n_a:
                pass_a(nxt, j)
            if j < len(t.blocks):
                pass_b(t, j)
        if nxt:
            end_a(nxt)
        finish(t, t.acc)


def _attn_scratch(s):
    return [pltpu.VMEM((2, s, LANES), BF16),
            pltpu.VMEM((2, s, 2 * LANES), BF16),
            pltpu.VMEM((4, TQ, s), F32)]


def _fox_kernel(q_ref, k_ref, v_ref, cum_ref, o_ref, qs_ref, vx_ref, s_ref):
    nq = q_ref.shape[1] // TQ
    pair = pl.program_id(1)
    _prep_streams(q_ref, v_ref, qs_ref, vx_ref, v_masked=True)
    ri = lax.broadcasted_iota(jnp.int32, (TQ, TQ), 0)
    ci = lax.broadcasted_iota(jnp.int32, (TQ, TQ), 1)
    causal_bias = jnp.where(ci <= ri, 0.0, NEG_INF).astype(F32)
    neg_cum = [-cum_ref[0, pl.ds(2 * pair + st, 1), :] for st in range(2)]
    def bias_fn(q0, st, k0, n):
        b = neg_cum[st][:, k0:k0 + n]
        if k0 + n <= q0:
            return b
        pad = [jnp.zeros((TQ, q0 - k0), F32)] if k0 < q0 else []
        return b + jnp.concatenate(pad + [causal_bias], axis=1)

    def finish(t, r):
        out = r[0][:, :LANES] / r[0][:, LANES:] + r[1][:, :LANES] / r[1][:, LANES:]
        o_ref[0, t.q0:t.q0 + TQ, :] = out.astype(BF16)

    tiles = [_QueryTile(qb, _key_blocks(0, (qb + 1) * TQ, TK),
                        functools.partial(bias_fn, qb * TQ)) for qb in range(nq)]
    _attend(tiles, qs_ref, k_ref, vx_ref, s_ref, finish)


def _fox_attention(fox, cum):
    b, s, _ = fox.shape
    n_pair = FOX_W // LANES
    col = lambda off: pl.BlockSpec((1, s, LANES), lambda i, p: (i, 0, off + p))
    return pl.pallas_call(
        _fox_kernel,
        grid=(b, n_pair),
        in_specs=[col(0), col(n_pair), col(2 * n_pair),
                  pl.BlockSpec((1, 8, s), lambda i, p: (i, 0, 0))],
        out_specs=col(0),
        out_shape=jax.ShapeDtypeStruct((b, s, FOX_W), BF16),
        scratch_shapes=_attn_scratch(s),
        compiler_params=_params(2),
        name="fox_attn",
    )(fox, fox, fox, cum)


def _diff_kernel(q_ref, k_ref, v_ref, slope_ref, lam_ref, linit_ref, g_ref, o_ref,
                 qs_ref, vx_ref, s_ref):
    nq = q_ref.shape[1] // TQ
    _prep_streams(q_ref, v_ref, qs_ref, vx_ref, v_masked=False)
    slope_t = jnp.tile(slope_ref[0], (1, TK // LANES))
    ri = lax.broadcasted_iota(jnp.int32, (TQ, TQ), 0)
    ci = lax.broadcasted_iota(jnp.int32, (TQ, TQ), 1)
    allowed = (ci // CHUNK) <= (ri // CHUNK)
    diag_bias = slope_t[:, :TQ] * jnp.where(ci <= ri, ci, 2 * ri - ci).astype(F32)
    diag_bias = jnp.where(allowed, diag_bias, NEG_INF)
    lane_k = lax.broadcasted_iota(jnp.int32, (1, TK), 1)
    lv = lam_ref[...]
    lam = (jnp.exp(jnp.sum(lv[0:1] * lv[1:2], axis=1, keepdims=True))
           - jnp.exp(jnp.sum(lv[2:3] * lv[3:4], axis=1, keepdims=True))
           + linit_ref[...])
    post = g_ref[...] * (1.0 - linit_ref[...])

    def bias_fn(q0, st, k0, n):
        n_past = min(n, q0 - k0)
        parts = []
        if n_past > 0:
            parts.append(jnp.broadcast_to(
                slope_t[:, :n_past] * (lane_k[:, :n_past] + (k0 - q0)).astype(F32),
                (TQ if n_past < n else 1, n_past)))
        if n_past < n:
            parts.append(diag_bias)
        return parts[0] if len(parts) == 1 else jnp.concatenate(parts, axis=1)

    def finish(t, r):
        o = (r[0][:, :LANES] / r[0][:, LANES:]
             - lam * (r[1][:, :LANES] / r[1][:, LANES:]))
        o = o * lax.rsqrt(jnp.mean(o * o, axis=-1, keepdims=True) + RMS_EPS) * post
        o_ref[0, t.q0:t.q0 + TQ, :] = o.astype(BF16)

    tiles = [_QueryTile(qb, _key_blocks(0, (qb + 1) * TQ, TK),
                        functools.partial(bias_fn, qb * TQ)) for qb in range(nq)]
    _attend(tiles, qs_ref, k_ref, vx_ref, s_ref, finish)


def _diff_attention(diff, slopes, lam_vecs, linit, g_diff):
    b, s, _ = diff.shape
    col = lambda off: pl.BlockSpec((1, s, LANES), lambda i, h: (i, 0, off + h))
    return pl.pallas_call(
        _diff_kernel,
        grid=(b, N_HEADS_DIFF),
        in_specs=[col(0), col(N_HEADS_DIFF), col(2 * N_HEADS_DIFF),
                  pl.BlockSpec((1, 1, LANES), lambda i, h: (h, 0, 0)),
                  _const_spec((4, HEAD_DIM)), _const_spec((1, LANES)),
                  _const_spec((1, LANES))],
        out_specs=col(0),
        out_shape=jax.ShapeDtypeStruct((b, s, DIFF_W), BF16),
        scratch_shapes=_attn_scratch(s),
        compiler_params=_params(2),
        name="diff_attn",
    )(diff, diff, diff, slopes, lam_vecs, linit, g_diff)


def _chk_kernel(q_ref, k_ref, v_ref, rel_ref, o_ref, qs_ref, vx_ref, s_ref, bias_ref):
    nq = q_ref.shape[1] // TQ
    pair = pl.program_id(1)

    @pl.when((pl.program_id(0) == 0) & (pair == 0))
    def _():
        u = lax.broadcasted_iota(jnp.int32, (REL_PAD, ROLL_W), 1)
        r = lax.broadcasted_iota(jnp.int32, (REL_PAD, ROLL_W), 0)
        u = jnp.where(u < CHK_WIN, u, u - ROLL_W)
        idx = jnp.clip(CHK_BAND - u, -(CHUNK - 1), MAX_REL_DIST) + (CHUNK - 1)
        onehot = (r == idx).astype(BF16)
        g_all = _dot3(rel_ref[...], onehot)
        ri = lax.broadcasted_iota(jnp.int32, (TQ, CHK_WIN), 0) // CHUNK
        ci = lax.broadcasted_iota(jnp.int32, (TQ, CHK_WIN), 1) // CHUNK
        band = (ci >= ri) & (ci <= ri + CHUNK_LOOKBACK)
        for h in range(N_HEADS):
            g = jnp.broadcast_to(g_all[h:h + 1], (TQ, ROLL_W))
            toep = pltpu.roll(g, 0, 1, stride=1, stride_axis=0)
            bias_ref[h] = jnp.where(band, toep[:, :CHK_WIN], NEG_INF)

    _prep_streams(q_ref, v_ref, qs_ref, vx_ref, v_masked=True)
    def bias_fn(q0, st, k0, n):
        w0 = k0 - (q0 - CHK_BAND)
        return bias_ref[2 * pair + st, :, w0:w0 + n]

    def finish(t, r):
        out = r[0][:, :LANES] / r[0][:, LANES:] + r[1][:, :LANES] / r[1][:, LANES:]
        o_ref[0, t.q0:t.q0 + TQ, :] = out.astype(BF16)

    tiles = [_QueryTile(qb, _key_blocks(max(0, qb * TQ - CHK_BAND), (qb + 1) * TQ, TK),
                        functools.partial(bias_fn, qb * TQ)) for qb in range(nq)]
    _attend(tiles, qs_ref, k_ref, vx_ref, s_ref, finish)


def _chk_attention(chk, rel_pad):
    b, s, _ = chk.shape
    n_pair = CHK_W // LANES
    col = lambda off: pl.BlockSpec((1, s, LANES), lambda i, p: (i, 0, off + p))
    return pl.pallas_call(
        _chk_kernel,
        grid=(b, n_pair),
        in_specs=[col(0), col(n_pair), col(2 * n_pair), _const_spec((8, REL_PAD))],
        out_specs=col(0),
        out_shape=jax.ShapeDtypeStruct((b, s, CHK_W), BF16),
        scratch_shapes=_attn_scratch(s)[:2] + [
            pltpu.VMEM((4, TQ, CHK_WIN), F32),
            pltpu.VMEM((N_HEADS, TQ, CHK_WIN), F32)],
        compiler_params=_params(2),
        name="chunk_attn",
    )(chk, chk, chk, rel_pad)


def _ffn_kernel(x_ref, fox_ref, diff_ref, chk_ref, wo_ref, g_ref, wup_ref, cw_ref, cb_ref,
                wdn_ref, gfin_ref, o_ref, carry_ref, act_ref, *, final_norm):
    tm = x_ref.shape[1]
    mix = jnp.concatenate([fox_ref[0], diff_ref[0], chk_ref[0]], axis=1)
    x1 = x_ref[0] + _dot(mix, wo_ref[...])
    h = _rms(x1, g_ref[...]).astype(BF16)

    @pl.when(pl.program_id(1) == 0)
    def _():
        carry_ref[...] = jnp.zeros_like(carry_ref)

    sub = lax.broadcasted_iota(jnp.int32, (8, FF_CHUNK), 0)

    def shifted(d, prev, n):
        r = pltpu.roll(d, n, 0)
        head = jnp.where(sub < n, pltpu.roll(prev, n, 0), r[:8])
        return jnp.concatenate([head, r[8:]], axis=0)

    for c in range(D_FF // FF_CHUNK):
        ys = []
        for part in range(2):
            c0 = part * D_FF + c * FF_CHUNK
            cols = slice(c0, c0 + FF_CHUNK)
            d = _dot(h, wup_ref[:, cols])
            prev = carry_ref[:, cols]
            carry_ref[:, cols] = d[tm - 8:]
            ys.append(cb_ref[:, cols]
                      + cw_ref[2:3, cols] * d
                      + cw_ref[1:2, cols] * shifted(d, prev, 1)
                      + cw_ref[0:1, cols] * shifted(d, prev, 2))
        hg = 0.5 * ys[1]
        act_ref[:, c * FF_CHUNK:(c + 1) * FF_CHUNK] = (
            (hg + hg * jnp.tanh(hg)) * ys[0]).astype(BF16)
    acc = x1 + _dot(act_ref[...], wdn_ref[...])
    if final_norm:
        acc = _rms(acc, gfin_ref[...])
    o_ref[0] = acc


def _out_ffn(x, fox_o, diff_o, chk_o, w_out, g_ffn, w_up, conv_w, conv_b, w_down, g_final,
             final_norm):
    b, s, d = x.shape
    grid = (b, s // TM_FFN)
    row_spec = lambda n: pl.BlockSpec((1, TM_FFN, n), lambda i, j: (i, j, 0))
    single = lambda shape: pl.BlockSpec(shape, lambda *_: (0,) * len(shape),
                                        pipeline_mode=pl.Buffered(1))
    return pl.pallas_call(
        functools.partial(_ffn_kernel, final_norm=final_norm),
        grid=grid,
        in_specs=[row_spec(d), row_spec(FOX_W), row_spec(DIFF_W), row_spec(CHK_W),
                  single((FOX_W + DIFF_W + CHK_W, d)), _const_spec((1, d)),
                  single((d, 2 * D_FF)),
                  _const_spec((3, 2 * D_FF)), _const_spec((1, 2 * D_FF)),
                  single((D_FF, d)), _const_spec((1, d))],
        out_specs=row_spec(d),
        out_shape=jax.ShapeDtypeStruct((b, s, d), F32),
        scratch_shapes=[pltpu.VMEM((8, 2 * D_FF), F32),
                        pltpu.VMEM((TM_FFN, D_FF), BF16)],
        compiler_params=_params(2),
        name="out_ffn",
    )(x, fox_o, diff_o, chk_o, w_out, g_ffn, w_up, conv_w, conv_b, w_down, g_final)


def kernel(x, g_mix, w_in, b_fox_f, diff_lambda, g_diff, rel_bias, w_out,
           g_ffn, w_ffn_in, conv_w, conv_b, w_ffn_out, g_final):
    depth = w_in.shape[0]
    d = x.shape[-1]
    o_ff = 3 * FOX_W
    o_rest = o_ff + N_HEADS
    slopes = jnp.asarray([2.0 ** (-8.0 * (i + 1) / N_HEADS_DIFF)
                          for i in range(N_HEADS_DIFF)], F32)
    slopes = jnp.broadcast_to(slopes[:, None, None], (N_HEADS_DIFF, 1, LANES))
    for l in range(depth):
        w_main = jnp.concatenate([w_in[l, :, :o_ff], w_in[l, :, o_rest:]], axis=1).astype(BF16)
        wff_t = jnp.zeros((8, d), BF16).at[:N_HEADS].set(
            w_in[l, :, o_ff:o_rest].T.astype(BF16))
        bff = jnp.zeros((8, 1), F32).at[:N_HEADS, 0].set(b_fox_f[l].astype(F32))
        fox, diff, chk, cum = _in_proj(x, g_mix[l][None], w_main, wff_t, bff)

        lambda_init = 0.8 - 0.6 * math.exp(-0.3 * l)
        linit = jnp.full((1, LANES), lambda_init, F32)
        rel_pad = jnp.zeros((8, REL_PAD), F32).at[:N_HEADS, :rel_bias.shape[2]].set(rel_bias[l])

        fox_o = _fox_attention(fox, cum)
        diff_o = _diff_attention(diff, slopes, diff_lambda[l].astype(F32), linit,
                                 g_diff[l][None].astype(F32))
        chk_o = _chk_attention(chk, rel_pad)

        x = _out_ffn(x, fox_o, diff_o, chk_o, w_out[l].astype(BF16), g_ffn[l][None],
                     w_ffn_in[l].astype(BF16), conv_w[l], conv_b[l][None],
                     w_ffn_out[l].astype(BF16), g_final[None], final_norm=(l == depth - 1))
    return x
```

```python
import functools
import math

import jax
import jax.numpy as jnp
from jax import lax
from jax.experimental import pallas as pl
from jax.experimental.pallas import tpu as pltpu

F32 = jnp.float32
BF16 = jnp.bfloat16

D_MODEL = 1024
HEAD_DIM = 64
CHUNK = 64
CHUNK_LOOKBACK = 8
MAX_REL_DIST = 128
D_FF = 2816
RMS_EPS = 1e-6
NEG_INF = -1e30
N_HEADS = 4
N_HEADS_DIFF = 2
FOX_W = 256
DIFF_W = 256
CHK_W = 256
LANES = 128
REL_PAD = 256

TM_IN = 1024
TM_FFN = 512
FF_CHUNK = 256
ROW_GROUP = 64
TQ = 256
TK = 512
CHK_BAND = CHUNK_LOOKBACK * CHUNK
TQ_CHK = 128
CHK_WIN = CHK_BAND + TQ_CHK
ROLL_W = 1024
VMEM_LIMIT = 56 * 1024 * 1024


def _dot(a, b):
    return jnp.dot(a, b, preferred_element_type=F32)


def _dot_nt(a, b):
    return lax.dot_general(a, b, (((1,), (1,)), ((), ())), preferred_element_type=F32)


def _split3(x):
    hi = x.astype(BF16)
    r1 = x - hi.astype(F32)
    mid = r1.astype(BF16)
    lo = (r1 - mid.astype(F32)).astype(BF16)
    return hi, mid, lo


def _dot3(x, t):
    hi, mid, lo = _split3(x)
    return _dot(hi, t) + _dot(mid, t) + _dot(lo, t)


def _rms(x, g):
    return x * lax.rsqrt(jnp.mean(x * x, axis=-1, keepdims=True) + RMS_EPS) * g


def _const_spec(shape):
    nd = len(shape)
    return pl.BlockSpec(shape, lambda *_: (0,) * nd)


def _layer_spec(shape, layer, single_buffer=False):
    nd = len(shape)
    return pl.BlockSpec((None,) + tuple(shape), lambda *_: (layer,) + (0,) * nd,
                        pipeline_mode=pl.Buffered(1) if single_buffer else None)


def _params(n_grid):
    return pltpu.CompilerParams(
        dimension_semantics=("arbitrary",) * n_grid, vmem_limit_bytes=VMEM_LIMIT)


def _in_proj_kernel(x_ref, g_ref, w_ref, wff_ref, bff_ref,
                    fox_ref, diff_ref, chk_ref, cum_ref, carry_ref):
    tm = x_ref.shape[1]
    h = _rms(x_ref[0], g_ref[...]).astype(BF16)

    ff = _dot_nt(wff_ref[...], h) + bff_ref[...]
    log_f = jax.nn.log_sigmoid(ff)

    @pl.when(pl.program_id(1) == 0)
    def _():
        carry_ref[...] = jnp.zeros_like(carry_ref)

    n_grp = tm // LANES
    row = lax.broadcasted_iota(jnp.int32, (LANES, 2 * LANES), 0)
    col = lax.broadcasted_iota(jnp.int32, (LANES, 2 * LANES), 1)
    tri_ones = ((row <= col) | (col >= LANES)).astype(BF16)
    grouped = jnp.concatenate(
        [log_f[:, g * LANES:(g + 1) * LANES] for g in range(n_grp)], axis=0)
    r = _dot(jnp.concatenate(_split3(grouped), axis=0), tri_ones)
    r = r[:8 * n_grp] + r[8 * n_grp:16 * n_grp] + r[16 * n_grp:]
    run = carry_ref[...]
    pieces = []
    for g in range(n_grp):
        pieces.append(r[8 * g:8 * g + 8, :LANES] + run)
        run = run + r[8 * g:8 * g + 8, LANES:]
    cum_ref[0] = jnp.concatenate(pieces, axis=1)
    carry_ref[...] = run

    n_fox, n_diff = fox_ref.shape[2], diff_ref.shape[2]
    fox_ref[0] = _dot(h, w_ref[:, :n_fox]).astype(BF16)
    diff_ref[0] = _dot(h, w_ref[:, n_fox:n_fox + n_diff]).astype(BF16)
    chk_ref[0] = _dot(h, w_ref[:, n_fox + n_diff:]).astype(BF16)


def _in_proj(x, layer, g, w_main, wff_t, bff):
    b, s, d = x.shape
    n_main = w_main.shape[2]
    n_fox, n_diff, n_chk = 3 * FOX_W, 3 * DIFF_W, 3 * CHK_W
    assert n_main == n_fox + n_diff + n_chk
    grid = (b, s // TM_IN)
    row_spec = lambda n: pl.BlockSpec((1, TM_IN, n), lambda i, j: (i, j, 0))
    return pl.pallas_call(
        _in_proj_kernel,
        grid=grid,
        in_specs=[row_spec(d), _layer_spec((1, d), layer), _layer_spec((d, n_main), layer),
                  _layer_spec((8, d), layer), _layer_spec((8, 1), layer)],
        out_specs=[row_spec(n_fox), row_spec(n_diff), row_spec(n_chk),
                   pl.BlockSpec((1, 8, TM_IN), lambda i, j: (i, 0, j))],
        out_shape=[jax.ShapeDtypeStruct((b, s, n_fox), BF16),
                   jax.ShapeDtypeStruct((b, s, n_diff), BF16),
                   jax.ShapeDtypeStruct((b, s, n_chk), BF16),
                   jax.ShapeDtypeStruct((b, 8, s), F32)],
        scratch_shapes=[pltpu.VMEM((8, LANES), F32)],
        compiler_params=_params(2),
        name="in_proj",
    )(x, g, w_main, wff_t, bff)


def _half_masks():
    lane = lax.broadcasted_iota(jnp.int32, (1, LANES), 1)
    return lane < HEAD_DIM, lane >= HEAD_DIM


def _prep_streams(q_ref, v_ref, qs_ref, vx_ref, v_masked):
    q = q_ref[0] * jnp.asarray(HEAD_DIM ** -0.5, BF16)
    v = v_ref[0]
    for st, m in enumerate(_half_masks()):
        qs_ref[st] = jnp.where(m, q, jnp.zeros_like(q))
        vx_ref[st, :, :LANES] = jnp.where(m, v, jnp.zeros_like(v)) if v_masked else v
        vx_ref[st, :, LANES:] = jnp.ones_like(v)


def _fold_max(s):
    m = s[:, :LANES]
    for j in range(1, s.shape[1] // LANES):
        m = jnp.maximum(m, s[:, j * LANES:(j + 1) * LANES])
    return m


def _key_blocks(k_lo, k_hi, width):
    return [(k0, min(width, k_hi - k0)) for k0 in range(k_lo, k_hi, width)]


class _QueryTile:
    def __init__(self, qb, blocks, bias_fn, tq=TQ):
        self.qb = qb
        self.tq = tq
        self.q0 = qb * tq
        self.blocks = blocks
        self.offs = [sum(n for _, n in blocks[:j]) for j in range(len(blocks))]
        self.bias_fn = bias_fn
        self.m = [None, None]
        self.mb = [None, None]
        self.acc = [None, None]


def _attend(tiles, qs_ref, k_ref, vx_ref, s_ref, finish):
    def s_slot(t, st):
        return s_ref.at[2 * (t.qb % 2) + st]

    def pass_a(t, j):
        k0, n = t.blocks[j]
        kt = k_ref[0, k0:k0 + n, :]
        for st in range(2):
            s = _dot_nt(qs_ref[st, t.q0:t.q0 + t.tq, :], kt) + t.bias_fn(st, k0, n)
            s_slot(t, st)[:, t.offs[j]:t.offs[j] + n] = s
            fm = _fold_max(s)
            t.m[st] = fm if t.m[st] is None else jnp.maximum(t.m[st], fm)

    def end_a(t):
        for st in range(2):
            t.mb[st] = jnp.broadcast_to(
                jnp.max(t.m[st], axis=-1, keepdims=True), (t.tq, LANES))

    def pass_b(t, j):
        k0, n = t.blocks[j]
        for st in range(2):
            s = s_slot(t, st)[:, t.offs[j]:t.offs[j] + n]
            e = jnp.exp(s - jnp.tile(t.mb[st], (1, n // LANES)))
            r = _dot(e.astype(BF16), vx_ref[st, k0:k0 + n, :])
            t.acc[st] = r if t.acc[st] is None else t.acc[st] + r

    for j in range(len(tiles[0].blocks)):
        pass_a(tiles[0], j)
    end_a(tiles[0])
    for i, t in enumerate(tiles):
        nxt = tiles[i + 1] if i + 1 < len(tiles) else None
        n_a = len(nxt.blocks) if nxt else 0
        for j in range(max(n_a, len(t.blocks))):
            if j < n_a:
                pass_a(nxt, j)
            if j < len(t.blocks):
                pass_b(t, j)
        if nxt:
            end_a(nxt)
        finish(t, t.acc)


def _attn_scratch(s):
    return [pltpu.VMEM((2, s, LANES), BF16),
            pltpu.VMEM((2, s, 2 * LANES), BF16),
            pltpu.VMEM((4, TQ, s), F32)]


def _fox_kernel(q_ref, k_ref, v_ref, cum_ref, o_ref, qs_ref, vx_ref, s_ref):
    nq = q_ref.shape[1] // TQ
    pair = pl.program_id(1)
    _prep_streams(q_ref, v_ref, qs_ref, vx_ref, v_masked=True)
    ri = lax.broadcasted_iota(jnp.int32, (TQ, TQ), 0)
    ci = lax.broadcasted_iota(jnp.int32, (TQ, TQ), 1)
    causal_bias = jnp.where(ci <= ri, 0.0, NEG_INF).astype(F32)
    neg_cum = [-cum_ref[0, pl.ds(2 * pair + st, 1), :] for st in range(2)]
    def bias_fn(q0, st, k0, n):
        b = neg_cum[st][:, k0:k0 + n]
        if k0 + n <= q0:
            return b
        pad = [jnp.zeros((TQ, q0 - k0), F32)] if k0 < q0 else []
        return b + jnp.concatenate(pad + [causal_bias], axis=1)

    def finish(t, r):
        out = r[0][:, :LANES] / r[0][:, LANES:] + r[1][:, :LANES] / r[1][:, LANES:]
        o_ref[0, t.q0:t.q0 + TQ, :] = out.astype(BF16)

    tiles = [_QueryTile(qb, _key_blocks(0, (qb + 1) * TQ, TK),
                        functools.partial(bias_fn, qb * TQ)) for qb in range(nq)]
    _attend(tiles, qs_ref, k_ref, vx_ref, s_ref, finish)


def _fox_attention(fox, cum):
    b, s, _ = fox.shape
    n_pair = FOX_W // LANES
    col = lambda off: pl.BlockSpec((1, s, LANES), lambda i, p: (i, 0, off + p))
    return pl.pallas_call(
        _fox_kernel,
        grid=(b, n_pair),
        in_specs=[col(0), col(n_pair), col(2 * n_pair),
                  pl.BlockSpec((1, 8, s), lambda i, p: (i, 0, 0))],
        out_specs=col(0),
        out_shape=jax.ShapeDtypeStruct((b, s, FOX_W), BF16),
        scratch_shapes=_attn_scratch(s),
        compiler_params=_params(2),
        name="fox_attn",
    )(fox, fox, fox, cum)


def _diff_kernel(q_ref, k_ref, v_ref, slope_ref, lam_ref, linit_ref, g_ref, o_ref,
                 qs_ref, vx_ref, s_ref):
    nq = q_ref.shape[1] // TQ
    _prep_streams(q_ref, v_ref, qs_ref, vx_ref, v_masked=False)
    slope_t = jnp.tile(slope_ref[0], (1, TK // LANES))
    ri = lax.broadcasted_iota(jnp.int32, (TQ, TQ), 0)
    ci = lax.broadcasted_iota(jnp.int32, (TQ, TQ), 1)
    allowed = (ci // CHUNK) <= (ri // CHUNK)
    diag_bias = slope_t[:, :TQ] * jnp.where(ci <= ri, ci, 2 * ri - ci).astype(F32)
    diag_bias = jnp.where(allowed, diag_bias, NEG_INF)
    lane_k = lax.broadcasted_iota(jnp.int32, (1, TK), 1)
    lv = lam_ref[...]
    lam = (jnp.exp(jnp.sum(lv[0:1] * lv[1:2], axis=1, keepdims=True))
           - jnp.exp(jnp.sum(lv[2:3] * lv[3:4], axis=1, keepdims=True))
           + linit_ref[...])
    post = g_ref[...] * (1.0 - linit_ref[...])

    def bias_fn(q0, st, k0, n):
        n_past = min(n, q0 - k0)
        parts = []
        if n_past > 0:
            parts.append(jnp.broadcast_to(
                slope_t[:, :n_past] * (lane_k[:, :n_past] + (k0 - q0)).astype(F32),
                (TQ if n_past < n else 1, n_past)))
        if n_past < n:
            parts.append(diag_bias)
        return parts[0] if len(parts) == 1 else jnp.concatenate(parts, axis=1)

    def finish(t, r):
        o = (r[0][:, :LANES] / r[0][:, LANES:]
             - lam * (r[1][:, :LANES] / r[1][:, LANES:]))
        o = o * lax.rsqrt(jnp.mean(o * o, axis=-1, keepdims=True) + RMS_EPS) * post
        o_ref[0, t.q0:t.q0 + TQ, :] = o.astype(BF16)

    tiles = [_QueryTile(qb, _key_blocks(0, (qb + 1) * TQ, TK),
                        functools.partial(bias_fn, qb * TQ)) for qb in range(nq)]
    _attend(tiles, qs_ref, k_ref, vx_ref, s_ref, finish)


def _diff_attention(diff, layer, slopes, lam_vecs, linit, g_diff):
    b, s, _ = diff.shape
    col = lambda off: pl.BlockSpec((1, s, LANES), lambda i, h: (i, 0, off + h))
    return pl.pallas_call(
        _diff_kernel,
        grid=(b, N_HEADS_DIFF),
        in_specs=[col(0), col(N_HEADS_DIFF), col(2 * N_HEADS_DIFF),
                  pl.BlockSpec((1, 1, LANES), lambda i, h: (h, 0, 0)),
                  _layer_spec((4, HEAD_DIM), layer), _layer_spec((1, LANES), layer),
                  _layer_spec((1, LANES), layer)],
        out_specs=col(0),
        out_shape=jax.ShapeDtypeStruct((b, s, DIFF_W), BF16),
        scratch_shapes=_attn_scratch(s),
        compiler_params=_params(2),
        name="diff_attn",
    )(diff, diff, diff, slopes, lam_vecs, linit, g_diff)


def _chk_kernel(q_ref, k_ref, v_ref, rel_ref, o_ref, qs_ref, vx_ref, s_ref, bias_ref):
    nq = q_ref.shape[1] // TQ_CHK
    pair = pl.program_id(1)

    @pl.when((pl.program_id(0) == 0) & (pair == 0))
    def _():
        u = lax.broadcasted_iota(jnp.int32, (REL_PAD, ROLL_W), 1)
        r = lax.broadcasted_iota(jnp.int32, (REL_PAD, ROLL_W), 0)
        u = jnp.where(u < CHK_WIN, u, u - ROLL_W)
        idx = jnp.clip(CHK_BAND - u, -(CHUNK - 1), MAX_REL_DIST) + (CHUNK - 1)
        onehot = (r == idx).astype(BF16)
        g_all = _dot3(rel_ref[...], onehot)
        ri = lax.broadcasted_iota(jnp.int32, (TQ_CHK, CHK_WIN), 0) // CHUNK
        ci = lax.broadcasted_iota(jnp.int32, (TQ_CHK, CHK_WIN), 1) // CHUNK
        band = (ci >= ri) & (ci <= ri + CHUNK_LOOKBACK)
        for h in range(N_HEADS):
            g = jnp.broadcast_to(g_all[h:h + 1], (TQ_CHK, ROLL_W))
            toep = pltpu.roll(g, 0, 1, stride=1, stride_axis=0)
            bias_ref[h] = jnp.where(band, toep[:, :CHK_WIN], NEG_INF)

    _prep_streams(q_ref, v_ref, qs_ref, vx_ref, v_masked=True)
    def bias_fn(q0, st, k0, n):
        w0 = k0 - (q0 - CHK_BAND)
        return bias_ref[2 * pair + st, :, w0:w0 + n]

    def finish(t, r):
        out = r[0][:, :LANES] / r[0][:, LANES:] + r[1][:, :LANES] / r[1][:, LANES:]
        o_ref[0, t.q0:t.q0 + TQ_CHK, :] = out.astype(BF16)

    tiles = [_QueryTile(qb, _key_blocks(max(0, qb * TQ_CHK - CHK_BAND), (qb + 1) * TQ_CHK, CHK_WIN),
                        functools.partial(bias_fn, qb * TQ_CHK), TQ_CHK) for qb in range(nq)]
    _attend(tiles, qs_ref, k_ref, vx_ref, s_ref, finish)


def _chk_attention(chk, layer, rel_pad):
    b, s, _ = chk.shape
    n_pair = CHK_W // LANES
    col = lambda off: pl.BlockSpec((1, s, LANES), lambda i, p: (i, 0, off + p))
    return pl.pallas_call(
        _chk_kernel,
        grid=(b, n_pair),
        in_specs=[col(0), col(n_pair), col(2 * n_pair), _layer_spec((8, REL_PAD), layer)],
        out_specs=col(0),
        out_shape=jax.ShapeDtypeStruct((b, s, CHK_W), BF16),
        scratch_shapes=_attn_scratch(s)[:2] + [
            pltpu.VMEM((4, TQ_CHK, CHK_WIN), F32),
            pltpu.VMEM((N_HEADS, TQ_CHK, CHK_WIN), F32)],
        compiler_params=_params(2),
        name="chunk_attn",
    )(chk, chk, chk, rel_pad)


def _ffn_kernel(x_ref, fox_ref, diff_ref, chk_ref, wo_ref, g_ref, wup_ref, cw_ref, cb_ref,
                wdn_ref, gfin_ref, o_ref, carry_ref, act_ref, t_ref, *, final_norm):
    tm, d_model = x_ref.shape[1], x_ref.shape[2]
    n_grp = tm // ROW_GROUP

    def transpose8(v, row0=0):
        n = v.shape[0]
        for j in range(d_model // LANES):
            t_ref[j, row0:row0 + n] = v[:, j * LANES:(j + 1) * LANES]
        cols = []
        for j in range(d_model // LANES):
            cols.append(jnp.concatenate(
                [t_ref[j, pl.ds(row0 + ROW_GROUP * grp + a, 8, stride=8), :]
                 for grp in range(n // ROW_GROUP) for a in range(8)], axis=0))
        return jnp.concatenate(cols, axis=1)

    x1_parts, h_parts = [], []
    for r0 in range(0, tm, tm // 2):
        rows = slice(r0, r0 + tm // 2)
        mix = jnp.concatenate([fox_ref[0, rows], diff_ref[0, rows], chk_ref[0, rows]], axis=1)
        x1_parts.append(transpose8(x_ref[0, rows] + _dot(mix, wo_ref[...]), r0))
        h_parts.append(_rms(x1_parts[-1], g_ref[...]).astype(BF16))
    x1 = jnp.concatenate(x1_parts, axis=0)
    h = jnp.concatenate(h_parts, axis=0)

    @pl.when(pl.program_id(1) == 0)
    def _():
        carry_ref[...] = jnp.zeros_like(carry_ref)

    sub = lax.broadcasted_iota(jnp.int32, (8, FF_CHUNK), 0)

    def conv_inputs(d, prev):
        r6, r7 = pltpu.roll(prev[:8], 1, 0), pltpu.roll(prev[8:], 1, 0)
        s1, s2 = [], []
        for grp in range(n_grp):
            g0 = ROW_GROUP * grp
            n6, n7 = pltpu.roll(d[g0 + 48:g0 + 56], 1, 0), pltpu.roll(d[g0 + 56:g0 + 64], 1, 0)
            h6, h7 = jnp.where(sub == 0, r6, n6), jnp.where(sub == 0, r7, n7)
            s1 += [h7, d[g0:g0 + 56]]
            s2 += [h6, h7, d[g0:g0 + 48]]
            r6, r7 = n6, n7
        return jnp.concatenate(s1, axis=0), jnp.concatenate(s2, axis=0)

    for c in range(D_FF // FF_CHUNK):
        ys = []
        for part in range(2):
            c0 = part * D_FF + c * FF_CHUNK
            cols = slice(c0, c0 + FF_CHUNK)
            if c == 0:
                d = jnp.concatenate([_dot(hp, wup_ref[:, cols]) for hp in h_parts], axis=0)
            else:
                d = _dot(h, wup_ref[:, cols])
            d1, d2 = conv_inputs(d, carry_ref[:, cols])
            carry_ref[:, cols] = d[tm - 16:]
            ys.append(cb_ref[:, cols] + cw_ref[2:3, cols] * d
                      + cw_ref[1:2, cols] * d1 + cw_ref[0:1, cols] * d2)
        hg = 0.5 * ys[1]
        act_ref[:, c * FF_CHUNK:(c + 1) * FF_CHUNK] = (
            (hg + hg * jnp.tanh(hg)) * ys[0]).astype(BF16)
    acc = x1 + _dot(act_ref[...], wdn_ref[...])
    if final_norm:
        acc = _rms(acc, gfin_ref[...])
    o_ref[0] = transpose8(acc)


def _out_ffn(x, fox_o, diff_o, chk_o, layer, w_out, g_ffn, w_up, conv_w, conv_b, w_down,
             g_final, final_norm):
    b, s, d = x.shape
    grid = (b, s // TM_FFN)
    row_spec = lambda n: pl.BlockSpec((1, TM_FFN, n), lambda i, j: (i, j, 0))
    single = lambda shape: _layer_spec(shape, layer, single_buffer=True)
    return pl.pallas_call(
        functools.partial(_ffn_kernel, final_norm=final_norm),
        grid=grid,
        in_specs=[row_spec(d), row_spec(FOX_W), row_spec(DIFF_W), row_spec(CHK_W),
                  single((FOX_W + DIFF_W + CHK_W, d)), _layer_spec((1, d), layer),
                  single((d, 2 * D_FF)),
                  _layer_spec((3, 2 * D_FF), layer), _layer_spec((1, 2 * D_FF), layer),
                  single((D_FF, d)), _const_spec((1, d))],
        out_specs=row_spec(d),
        out_shape=jax.ShapeDtypeStruct((b, s, d), F32),
        scratch_shapes=[pltpu.VMEM((16, 2 * D_FF), F32),
                        pltpu.VMEM((TM_FFN, D_FF), BF16),
                        pltpu.VMEM((d // LANES, TM_FFN, LANES), F32)],
        compiler_params=_params(2),
        name="out_ffn",
    )(x, fox_o, diff_o, chk_o, w_out, g_ffn, w_up, conv_w, conv_b, w_down, g_final)


def kernel(x, g_mix, w_in, b_fox_f, diff_lambda, g_diff, rel_bias, w_out,
           g_ffn, w_ffn_in, conv_w, conv_b, w_ffn_out, g_final):
    depth = w_in.shape[0]
    d = x.shape[-1]
    o_ff = 3 * FOX_W
    o_rest = o_ff + N_HEADS
    slopes = jnp.asarray([2.0 ** (-8.0 * (i + 1) / N_HEADS_DIFF)
                          for i in range(N_HEADS_DIFF)], F32)
    slopes = jnp.broadcast_to(slopes[:, None, None], (N_HEADS_DIFF, 1, LANES))
    w_main = jnp.concatenate([w_in[:, :, :o_ff], w_in[:, :, o_rest:]], axis=2).astype(BF16)
    wff_t = jnp.zeros((depth, 8, d), BF16).at[:, :N_HEADS].set(
        jnp.swapaxes(w_in[:, :, o_ff:o_rest], 1, 2).astype(BF16))
    bff = jnp.zeros((depth, 8, 1), F32).at[:, :N_HEADS, 0].set(b_fox_f.astype(F32))
    linit = jnp.stack([jnp.full((1, LANES), 0.8 - 0.6 * math.exp(-0.3 * l), F32)
                       for l in range(depth)])
    rel_pad = jnp.zeros((depth, 8, REL_PAD), F32).at[:, :N_HEADS, :rel_bias.shape[2]].set(
        rel_bias.astype(F32))
    w_out_b, w_up_b, w_down_b = (w.astype(BF16) for w in (w_out, w_ffn_in, w_ffn_out))

    for l in range(depth):
        fox, diff, chk, cum = _in_proj(x, l, g_mix[:, None], w_main, wff_t, bff)
        fox_o = _fox_attention(fox, cum)
        diff_o = _diff_attention(diff, l, slopes, diff_lambda.astype(F32), linit,
                                 g_diff[:, None].astype(F32))
        chk_o = _chk_attention(chk, l, rel_pad)
        x = _out_ffn(x, fox_o, diff_o, chk_o, l, w_out_b, g_ffn[:, None], w_up_b, conv_w,
                     conv_b[:, None], w_down_b, g_final[None], final_norm=(l == depth - 1))
    return x
```

```python
import functools
import math

import jax
import jax.numpy as jnp
from jax import lax
from jax.experimental import pallas as pl
from jax.experimental.pallas import tpu as pltpu

F32 = jnp.float32
BF16 = jnp.bfloat16

D_MODEL = 1024
HEAD_DIM = 64
CHUNK = 64
CHUNK_LOOKBACK = 8
MAX_REL_DIST = 128
D_FF = 2816
RMS_EPS = 1e-6
NEG_INF = -1e30
N_HEADS = 4
N_HEADS_DIFF = 2
FOX_W = 256
DIFF_W = 256
CHK_W = 256
LANES = 128
REL_PAD = 256

TM_IN = 1024
TM_FFN = 1024
FF_CHUNK = 256
ROW_GROUP = 64
TQ = 256
TK = 512
CHK_BAND = CHUNK_LOOKBACK * CHUNK
TQ_CHK = 128
CHK_WIN = CHK_BAND + TQ_CHK
ROLL_W = 1024
VMEM_LIMIT = 56 * 1024 * 1024


def _dot(a, b):
    return jnp.dot(a, b, preferred_element_type=F32)


def _dot_nt(a, b):
    return lax.dot_general(a, b, (((1,), (1,)), ((), ())), preferred_element_type=F32)


def _split3(x):
    hi = x.astype(BF16)
    r1 = x - hi.astype(F32)
    mid = r1.astype(BF16)
    lo = (r1 - mid.astype(F32)).astype(BF16)
    return hi, mid, lo


def _dot3(x, t):
    hi, mid, lo = _split3(x)
    return _dot(hi, t) + _dot(mid, t) + _dot(lo, t)


def _rms(x, g):
    return x * lax.rsqrt(jnp.mean(x * x, axis=-1, keepdims=True) + RMS_EPS) * g


def _const_spec(shape):
    nd = len(shape)
    return pl.BlockSpec(shape, lambda *_: (0,) * nd)


def _layer_spec(shape, layer, single_buffer=False):
    nd = len(shape)
    return pl.BlockSpec((None,) + tuple(shape), lambda *_: (layer,) + (0,) * nd,
                        pipeline_mode=pl.Buffered(1) if single_buffer else None)


def _params(n_grid):
    return pltpu.CompilerParams(
        dimension_semantics=("arbitrary",) * n_grid, vmem_limit_bytes=VMEM_LIMIT)


def _in_proj_kernel(x_ref, g_ref, w_ref, wff_ref, bff_ref,
                    fox_ref, diff_ref, chk_ref, cum_ref, carry_ref):
    tm = x_ref.shape[1]
    h = _rms(x_ref[0], g_ref[...]).astype(BF16)

    ff = _dot_nt(wff_ref[...], h) + bff_ref[...]
    log_f = jax.nn.log_sigmoid(ff)

    @pl.when(pl.program_id(1) == 0)
    def _():
        carry_ref[...] = jnp.zeros_like(carry_ref)

    n_grp = tm // LANES
    row = lax.broadcasted_iota(jnp.int32, (LANES, 2 * LANES), 0)
    col = lax.broadcasted_iota(jnp.int32, (LANES, 2 * LANES), 1)
    tri_ones = ((row <= col) | (col >= LANES)).astype(BF16)
    grouped = jnp.concatenate(
        [log_f[:, g * LANES:(g + 1) * LANES] for g in range(n_grp)], axis=0)
    r = _dot(jnp.concatenate(_split3(grouped), axis=0), tri_ones)
    r = r[:8 * n_grp] + r[8 * n_grp:16 * n_grp] + r[16 * n_grp:]
    run = carry_ref[...]
    pieces = []
    for g in range(n_grp):
        pieces.append(r[8 * g:8 * g + 8, :LANES] + run)
        run = run + r[8 * g:8 * g + 8, LANES:]
    cum_ref[0] = jnp.concatenate(pieces, axis=1)
    carry_ref[...] = run

    n_fox, n_diff = fox_ref.shape[2], diff_ref.shape[2]
    fox_ref[0] = _dot(h, w_ref[:, :n_fox]).astype(BF16)
    diff_ref[0] = _dot(h, w_ref[:, n_fox:n_fox + n_diff]).astype(BF16)
    chk_ref[0] = _dot(h, w_ref[:, n_fox + n_diff:]).astype(BF16)


def _in_proj(x, layer, g, w_main, wff_t, bff):
    b, s, d = x.shape
    n_main = w_main.shape[2]
    n_fox, n_diff, n_chk = 3 * FOX_W, 3 * DIFF_W, 3 * CHK_W
    assert n_main == n_fox + n_diff + n_chk
    grid = (b, s // TM_IN)
    row_spec = lambda n: pl.BlockSpec((1, TM_IN, n), lambda i, j: (i, j, 0))
    return pl.pallas_call(
        _in_proj_kernel,
        grid=grid,
        in_specs=[row_spec(d), _layer_spec((1, d), layer), _layer_spec((d, n_main), layer),
                  _layer_spec((8, d), layer), _layer_spec((8, 1), layer)],
        out_specs=[row_spec(n_fox), row_spec(n_diff), row_spec(n_chk),
                   pl.BlockSpec((1, 8, TM_IN), lambda i, j: (i, 0, j))],
        out_shape=[jax.ShapeDtypeStruct((b, s, n_fox), BF16),
                   jax.ShapeDtypeStruct((b, s, n_diff), BF16),
                   jax.ShapeDtypeStruct((b, s, n_chk), BF16),
                   jax.ShapeDtypeStruct((b, 8, s), F32)],
        scratch_shapes=[pltpu.VMEM((8, LANES), F32)],
        compiler_params=_params(2),
        name="in_proj",
    )(x, g, w_main, wff_t, bff)


def _half_masks():
    lane = lax.broadcasted_iota(jnp.int32, (1, LANES), 1)
    return lane < HEAD_DIM, lane >= HEAD_DIM


def _prep_streams(q_ref, v_ref, qs_ref, vx_ref, v_masked):
    for grp in range(q_ref.shape[2] // LANES):
        lanes = slice(grp * LANES, (grp + 1) * LANES)
        q = q_ref[0, :, lanes] * jnp.asarray(HEAD_DIM ** -0.5, BF16)
        v = v_ref[0, :, lanes]
        for st, m in enumerate(_half_masks()):
            qs_ref[2 * grp + st] = jnp.where(m, q, jnp.zeros_like(q))
            vx_ref[2 * grp + st, :, :LANES] = (
                jnp.where(m, v, jnp.zeros_like(v)) if v_masked else v)
            vx_ref[2 * grp + st, :, LANES:] = jnp.ones_like(v)


def _fold_max(s):
    m = s[:, :LANES]
    for j in range(1, s.shape[1] // LANES):
        m = jnp.maximum(m, s[:, j * LANES:(j + 1) * LANES])
    return m


def _key_blocks(k_lo, k_hi, width):
    return [(k0, min(width, k_hi - k0)) for k0 in range(k_lo, k_hi, width)]


class _QueryTile:
    def __init__(self, grp, qb, blocks, bias_fn, tq=TQ):
        self.grp = grp
        self.tq = tq
        self.q0 = qb * tq
        self.blocks = blocks
        self.offs = [sum(n for _, n in blocks[:j]) for j in range(len(blocks))]
        self.bias_fn = bias_fn
        self.m = [None, None]
        self.mb = [None, None]
        self.acc = [None, None]


def _attend(tiles, qs_ref, k_ref, vx_ref, s_ref, finish):
    for i, t in enumerate(tiles):
        t.slot = 2 * (i % 2)

    def pass_a(t, j):
        k0, n = t.blocks[j]
        kt = k_ref[0, k0:k0 + n, t.grp * LANES:(t.grp + 1) * LANES]
        for st in range(2):
            s = (_dot_nt(qs_ref[2 * t.grp + st, t.q0:t.q0 + t.tq, :], kt)
                 + t.bias_fn(st, k0, n))
            s_ref[t.slot + st, :, t.offs[j]:t.offs[j] + n] = s
            fm = _fold_max(s)
            t.m[st] = fm if t.m[st] is None else jnp.maximum(t.m[st], fm)

    def end_a(t):
        for st in range(2):
            t.mb[st] = jnp.broadcast_to(
                jnp.max(t.m[st], axis=-1, keepdims=True), (t.tq, LANES))

    def pass_b(t, j):
        k0, n = t.blocks[j]
        for st in range(2):
            s = s_ref[t.slot + st, :, t.offs[j]:t.offs[j] + n]
            e = jnp.exp(s - jnp.tile(t.mb[st], (1, n // LANES)))
            r = _dot(e.astype(BF16), vx_ref[2 * t.grp + st, k0:k0 + n, :])
            t.acc[st] = r if t.acc[st] is None else t.acc[st] + r

    for j in range(len(tiles[0].blocks)):
        pass_a(tiles[0], j)
    end_a(tiles[0])
    for i, t in enumerate(tiles):
        nxt = tiles[i + 1] if i + 1 < len(tiles) else None
        n_a = len(nxt.blocks) if nxt else 0
        for j in range(max(n_a, len(t.blocks))):
            if j < n_a:
                pass_a(nxt, j)
            if j < len(t.blocks):
                pass_b(t, j)
        if nxt:
            end_a(nxt)
        finish(t, t.acc)


def _attn_scratch(s, width, tq, s_cols):
    n_stream = 2 * (width // LANES)
    return [pltpu.VMEM((n_stream, s, LANES), BF16),
            pltpu.VMEM((n_stream, s, 2 * LANES), BF16),
            pltpu.VMEM((4, tq, s_cols), F32)]


def _qkv_specs(s, width):
    return [pl.BlockSpec((1, s, width), lambda i, c=c: (i, 0, c)) for c in range(3)]


def _fox_kernel(q_ref, k_ref, v_ref, cum_ref, o_ref, qs_ref, vx_ref, s_ref):
    nq = q_ref.shape[1] // TQ
    _prep_streams(q_ref, v_ref, qs_ref, vx_ref, v_masked=True)
    ri = lax.broadcasted_iota(jnp.int32, (TQ, TQ), 0)
    ci = lax.broadcasted_iota(jnp.int32, (TQ, TQ), 1)
    causal_bias = jnp.where(ci <= ri, 0.0, NEG_INF).astype(F32)
    neg_cum = [-cum_ref[0, h:h + 1, :] for h in range(N_HEADS)]

    def bias_fn(grp, q0, st, k0, n):
        b = neg_cum[2 * grp + st][:, k0:k0 + n]
        if k0 + n <= q0:
            return b
        pad = [jnp.zeros((TQ, q0 - k0), F32)] if k0 < q0 else []
        return b + jnp.concatenate(pad + [causal_bias], axis=1)

    def finish(t, r):
        out = r[0][:, :LANES] / r[0][:, LANES:] + r[1][:, :LANES] / r[1][:, LANES:]
        o_ref[0, t.q0:t.q0 + TQ, t.grp * LANES:(t.grp + 1) * LANES] = out.astype(BF16)

    tiles = [_QueryTile(grp, qb, _key_blocks(0, (qb + 1) * TQ, TK),
                        functools.partial(bias_fn, grp, qb * TQ))
             for grp in range(FOX_W // LANES) for qb in range(nq)]
    _attend(tiles, qs_ref, k_ref, vx_ref, s_ref, finish)


def _fox_attention(fox, cum):
    b, s, _ = fox.shape
    return pl.pallas_call(
        _fox_kernel,
        grid=(b,),
        in_specs=_qkv_specs(s, FOX_W) + [pl.BlockSpec((1, 8, s), lambda i: (i, 0, 0))],
        out_specs=pl.BlockSpec((1, s, FOX_W), lambda i: (i, 0, 0)),
        out_shape=jax.ShapeDtypeStruct((b, s, FOX_W), BF16),
        scratch_shapes=_attn_scratch(s, FOX_W, TQ, s),
        compiler_params=_params(1),
        name="fox_attn",
    )(fox, fox, fox, cum)


def _diff_kernel(q_ref, k_ref, v_ref, slope_ref, lam_ref, linit_ref, g_ref, o_ref,
                 qs_ref, vx_ref, s_ref):
    nq = q_ref.shape[1] // TQ
    _prep_streams(q_ref, v_ref, qs_ref, vx_ref, v_masked=False)
    ri = lax.broadcasted_iota(jnp.int32, (TQ, TQ), 0)
    ci = lax.broadcasted_iota(jnp.int32, (TQ, TQ), 1)
    allowed = (ci // CHUNK) <= (ri // CHUNK)
    rel = jnp.where(ci <= ri, ci, 2 * ri - ci).astype(F32)
    lane_k = lax.broadcasted_iota(jnp.int32, (1, TK), 1)
    slope_t, diag_bias = [], []
    for h in range(N_HEADS_DIFF):
        slope_t.append(jnp.tile(slope_ref[h], (1, TK // LANES)))
        diag_bias.append(jnp.where(allowed, slope_t[h][:, :TQ] * rel, NEG_INF))
    lv = lam_ref[...]
    lam = (jnp.exp(jnp.sum(lv[0:1] * lv[1:2], axis=1, keepdims=True))
           - jnp.exp(jnp.sum(lv[2:3] * lv[3:4], axis=1, keepdims=True))
           + linit_ref[...])
    post = g_ref[...] * (1.0 - linit_ref[...])

    def bias_fn(grp, q0, st, k0, n):
        n_past = min(n, q0 - k0)
        parts = []
        if n_past > 0:
            parts.append(jnp.broadcast_to(
                slope_t[grp][:, :n_past] * (lane_k[:, :n_past] + (k0 - q0)).astype(F32),
                (TQ if n_past < n else 1, n_past)))
        if n_past < n:
            parts.append(diag_bias[grp])
        return parts[0] if len(parts) == 1 else jnp.concatenate(parts, axis=1)

    def finish(t, r):
        o = (r[0][:, :LANES] / r[0][:, LANES:]
             - lam * (r[1][:, :LANES] / r[1][:, LANES:]))
        o = o * lax.rsqrt(jnp.mean(o * o, axis=-1, keepdims=True) + RMS_EPS) * post
        o_ref[0, t.q0:t.q0 + TQ, t.grp * LANES:(t.grp + 1) * LANES] = o.astype(BF16)

    tiles = [_QueryTile(grp, qb, _key_blocks(0, (qb + 1) * TQ, TK),
                        functools.partial(bias_fn, grp, qb * TQ))
             for grp in range(N_HEADS_DIFF) for qb in range(nq)]
    _attend(tiles, qs_ref, k_ref, vx_ref, s_ref, finish)


def _diff_attention(diff, layer, slopes, lam_vecs, linit, g_diff):
    b, s, _ = diff.shape
    return pl.pallas_call(
        _diff_kernel,
        grid=(b,),
        in_specs=_qkv_specs(s, DIFF_W) + [
            _const_spec((N_HEADS_DIFF, 1, LANES)),
            _layer_spec((4, HEAD_DIM), layer), _layer_spec((1, LANES), layer),
            _layer_spec((1, LANES), layer)],
        out_specs=pl.BlockSpec((1, s, DIFF_W), lambda i: (i, 0, 0)),
        out_shape=jax.ShapeDtypeStruct((b, s, DIFF_W), BF16),
        scratch_shapes=_attn_scratch(s, DIFF_W, TQ, s),
        compiler_params=_params(1),
        name="diff_attn",
    )(diff, diff, diff, slopes, lam_vecs, linit, g_diff)


def _chk_kernel(q_ref, k_ref, v_ref, rel_ref, o_ref, qs_ref, vx_ref, s_ref, bias_ref):
    nq = q_ref.shape[1] // TQ_CHK

    @pl.when(pl.program_id(0) == 0)
    def _():
        u = lax.broadcasted_iota(jnp.int32, (REL_PAD, ROLL_W), 1)
        r = lax.broadcasted_iota(jnp.int32, (REL_PAD, ROLL_W), 0)
        u = jnp.where(u < CHK_WIN, u, u - ROLL_W)
        idx = jnp.clip(CHK_BAND - u, -(CHUNK - 1), MAX_REL_DIST) + (CHUNK - 1)
        onehot = (r == idx).astype(BF16)
        g_all = _dot3(rel_ref[...], onehot)
        ri = lax.broadcasted_iota(jnp.int32, (TQ_CHK, CHK_WIN), 0) // CHUNK
        ci = lax.broadcasted_iota(jnp.int32, (TQ_CHK, CHK_WIN), 1) // CHUNK
        band = (ci >= ri) & (ci <= ri + CHUNK_LOOKBACK)
        for h in range(N_HEADS):
            g = jnp.broadcast_to(g_all[h:h + 1], (TQ_CHK, ROLL_W))
            toep = pltpu.roll(g, 0, 1, stride=1, stride_axis=0)
            bias_ref[h] = jnp.where(band, toep[:, :CHK_WIN], NEG_INF)

    _prep_streams(q_ref, v_ref, qs_ref, vx_ref, v_masked=True)

    def bias_fn(grp, q0, st, k0, n):
        w0 = k0 - (q0 - CHK_BAND)
        return bias_ref[2 * grp + st, :, w0:w0 + n]

    def finish(t, r):
        out = r[0][:, :LANES] / r[0][:, LANES:] + r[1][:, :LANES] / r[1][:, LANES:]
        o_ref[0, t.q0:t.q0 + TQ_CHK, t.grp * LANES:(t.grp + 1) * LANES] = out.astype(BF16)

    tiles = [_QueryTile(grp, qb,
                        _key_blocks(max(0, qb * TQ_CHK - CHK_BAND), (qb + 1) * TQ_CHK, CHK_WIN),
                        functools.partial(bias_fn, grp, qb * TQ_CHK), TQ_CHK)
             for grp in range(CHK_W // LANES) for qb in range(nq)]
    _attend(tiles, qs_ref, k_ref, vx_ref, s_ref, finish)


def _chk_attention(chk, layer, rel_pad):
    b, s, _ = chk.shape
    return pl.pallas_call(
        _chk_kernel,
        grid=(b,),
        in_specs=_qkv_specs(s, CHK_W) + [_layer_spec((8, REL_PAD), layer)],
        out_specs=pl.BlockSpec((1, s, CHK_W), lambda i: (i, 0, 0)),
        out_shape=jax.ShapeDtypeStruct((b, s, CHK_W), BF16),
        scratch_shapes=_attn_scratch(s, CHK_W, TQ_CHK, CHK_WIN) + [
            pltpu.VMEM((N_HEADS, TQ_CHK, CHK_WIN), F32)],
        compiler_params=_params(1),
        name="chunk_attn",
    )(chk, chk, chk, rel_pad)


def _ffn_kernel(x_ref, fox_ref, diff_ref, chk_ref, wo_ref, g_ref, wup_ref, cw_ref, cb_ref,
                wdn_ref, gfin_ref, o_ref, carry_ref, act_ref, t_ref, *, final_norm):
    tm, d_model = x_ref.shape[1], x_ref.shape[2]
    n_grp = tm // ROW_GROUP

    def transpose8(v, row0=0):
        n = v.shape[0]
        for j in range(d_model // LANES):
            t_ref[j, row0:row0 + n] = v[:, j * LANES:(j + 1) * LANES]
        cols = []
        for j in range(d_model // LANES):
            cols.append(jnp.concatenate(
                [t_ref[j, pl.ds(row0 + ROW_GROUP * grp + a, 8, stride=8), :]
                 for grp in range(n // ROW_GROUP) for a in range(8)], axis=0))
        return jnp.concatenate(cols, axis=1)

    x1_parts, h_parts = [], []
    for r0 in range(0, tm, tm // 2):
        rows = slice(r0, r0 + tm // 2)
        mix = jnp.concatenate([fox_ref[0, rows], diff_ref[0, rows], chk_ref[0, rows]], axis=1)
        x1_parts.append(transpose8(x_ref[0, rows] + _dot(mix, wo_ref[...]), r0))
        h_parts.append(_rms(x1_parts[-1], g_ref[...]).astype(BF16))
    x1 = jnp.concatenate(x1_parts, axis=0)
    h = jnp.concatenate(h_parts, axis=0)

    @pl.when(pl.program_id(1) == 0)
    def _():
        carry_ref[...] = jnp.zeros_like(carry_ref)

    sub = lax.broadcasted_iota(jnp.int32, (8, FF_CHUNK), 0)

    def conv_inputs(d, prev):
        r6, r7 = pltpu.roll(prev[:8], 1, 0), pltpu.roll(prev[8:], 1, 0)
        s1, s2 = [], []
        for grp in range(n_grp):
            g0 = ROW_GROUP * grp
            n6, n7 = pltpu.roll(d[g0 + 48:g0 + 56], 1, 0), pltpu.roll(d[g0 + 56:g0 + 64], 1, 0)
            h6, h7 = jnp.where(sub == 0, r6, n6), jnp.where(sub == 0, r7, n7)
            s1 += [h7, d[g0:g0 + 56]]
            s2 += [h6, h7, d[g0:g0 + 48]]
            r6, r7 = n6, n7
        return jnp.concatenate(s1, axis=0), jnp.concatenate(s2, axis=0)

    for c in range(D_FF // FF_CHUNK):
        ys = []
        for part in range(2):
            c0 = part * D_FF + c * FF_CHUNK
            cols = slice(c0, c0 + FF_CHUNK)
            if c == 0:
                d = jnp.concatenate([_dot(hp, wup_ref[:, cols]) for hp in h_parts], axis=0)
            else:
                d = _dot(h, wup_ref[:, cols])
            d1, d2 = conv_inputs(d, carry_ref[:, cols])
            carry_ref[:, cols] = d[tm - 16:]
            ys.append(cb_ref[:, cols] + cw_ref[2:3, cols] * d
                      + cw_ref[1:2, cols] * d1 + cw_ref[0:1, cols] * d2)
        hg = 0.5 * ys[1]
        act_ref[:, c * FF_CHUNK:(c + 1) * FF_CHUNK] = (
            (hg + hg * jnp.tanh(hg)) * ys[0]).astype(BF16)
    acc = x1 + _dot(act_ref[...], wdn_ref[...])
    if final_norm:
        acc = _rms(acc, gfin_ref[...])
    o_ref[0] = transpose8(acc)


def _out_ffn(x, fox_o, diff_o, chk_o, layer, w_out, g_ffn, w_up, conv_w, conv_b, w_down,
             g_final, final_norm):
    b, s, d = x.shape
    grid = (b, s // TM_FFN)
    row_spec = lambda n: pl.BlockSpec((1, TM_FFN, n), lambda i, j: (i, j, 0))
    single = lambda shape: _layer_spec(shape, layer, single_buffer=True)
    return pl.pallas_call(
        functools.partial(_ffn_kernel, final_norm=final_norm),
        grid=grid,
        in_specs=[row_spec(d), row_spec(FOX_W), row_spec(DIFF_W), row_spec(CHK_W),
                  single((FOX_W + DIFF_W + CHK_W, d)), _layer_spec((1, d), layer),
                  single((d, 2 * D_FF)),
                  _layer_spec((3, 2 * D_FF), layer), _layer_spec((1, 2 * D_FF), layer),
                  single((D_FF, d)), _const_spec((1, d))],
        out_specs=row_spec(d),
        out_shape=jax.ShapeDtypeStruct((b, s, d), F32),
        scratch_shapes=[pltpu.VMEM((16, 2 * D_FF), F32),
                        pltpu.VMEM((TM_FFN, D_FF), BF16),
                        pltpu.VMEM((d // LANES, TM_FFN, LANES), F32)],
        compiler_params=_params(2),
        name="out_ffn",
    )(x, fox_o, diff_o, chk_o, w_out, g_ffn, w_up, conv_w, conv_b, w_down, g_final)


def kernel(x, g_mix, w_in, b_fox_f, diff_lambda, g_diff, rel_bias, w_out,
           g_ffn, w_ffn_in, conv_w, conv_b, w_ffn_out, g_final):
    depth = w_in.shape[0]
    d = x.shape[-1]
    o_ff = 3 * FOX_W
    o_rest = o_ff + N_HEADS
    slopes = jnp.asarray([2.0 ** (-8.0 * (i + 1) / N_HEADS_DIFF)
                          for i in range(N_HEADS_DIFF)], F32)
    slopes = jnp.broadcast_to(slopes[:, None, None], (N_HEADS_DIFF, 1, LANES))
    w_main = jnp.concatenate([w_in[:, :, :o_ff], w_in[:, :, o_rest:]], axis=2).astype(BF16)
    wff_t = jnp.zeros((depth, 8, d), BF16).at[:, :N_HEADS].set(
        jnp.swapaxes(w_in[:, :, o_ff:o_rest], 1, 2).astype(BF16))
    bff = jnp.zeros((depth, 8, 1), F32).at[:, :N_HEADS, 0].set(b_fox_f.astype(F32))
    linit = jnp.stack([jnp.full((1, LANES), 0.8 - 0.6 * math.exp(-0.3 * l), F32)
                       for l in range(depth)])
    rel_pad = jnp.zeros((depth, 8, REL_PAD), F32).at[:, :N_HEADS, :rel_bias.shape[2]].set(
        rel_bias.astype(F32))
    w_out_b, w_up_b, w_down_b = (w.astype(BF16) for w in (w_out, w_ffn_in, w_ffn_out))

    for l in range(depth):
        fox, diff, chk, cum = _in_proj(x, l, g_mix[:, None], w_main, wff_t, bff)
        fox_o = _fox_attention(fox, cum)
        diff_o = _diff_attention(diff, l, slopes, diff_lambda.astype(F32), linit,
                                 g_diff[:, None].astype(F32))
        chk_o = _chk_attention(chk, l, rel_pad)
        x = _out_ffn(x, fox_o, diff_o, chk_o, l, w_out_b, g_ffn[:, None], w_up_b, conv_w,
                     conv_b[:, None], w_down_b, g_final[None], final_norm=(l == depth - 1))
    return x
```

```python
import functools
import math

import jax
import jax.numpy as jnp
from jax import lax
from jax.experimental import pallas as pl
from jax.experimental.pallas import tpu as pltpu

F32 = jnp.float32
BF16 = jnp.bfloat16

D_MODEL = 1024
HEAD_DIM = 64
CHUNK = 64
CHUNK_LOOKBACK = 8
MAX_REL_DIST = 128
D_FF = 2816
RMS_EPS = 1e-6
NEG_INF = -1e30
N_HEADS = 4
N_HEADS_DIFF = 2
FOX_W = 256
DIFF_W = 256
CHK_W = 256
LANES = 128
REL_PAD = 256

TM_IN = 1024
TM_FFN = 1024
FF_CHUNK = 256
ROW_GROUP = 64
TQ = 256
TK = 512
CHK_BAND = CHUNK_LOOKBACK * CHUNK
TQ_CHK = 128
CHK_WIN = CHK_BAND + TQ_CHK
ROLL_W = 1024
VMEM_LIMIT = 56 * 1024 * 1024


def _dot(a, b):
    return jnp.dot(a, b, preferred_element_type=F32)


def _dot_nt(a, b):
    return lax.dot_general(a, b, (((1,), (1,)), ((), ())), preferred_element_type=F32)


def _split3(x):
    hi = x.astype(BF16)
    r1 = x - hi.astype(F32)
    mid = r1.astype(BF16)
    lo = (r1 - mid.astype(F32)).astype(BF16)
    return hi, mid, lo


def _dot3(x, t):
    hi, mid, lo = _split3(x)
    return _dot(hi, t) + _dot(mid, t) + _dot(lo, t)


def _rms(x, g):
    return x * lax.rsqrt(jnp.mean(x * x, axis=-1, keepdims=True) + RMS_EPS) * g


def _const_spec(shape):
    nd = len(shape)
    return pl.BlockSpec(shape, lambda *_: (0,) * nd)


def _layer_spec(shape, layer, single_buffer=False):
    nd = len(shape)
    return pl.BlockSpec((None,) + tuple(shape), lambda *_: (layer,) + (0,) * nd,
                        pipeline_mode=pl.Buffered(1) if single_buffer else None)


def _params(n_grid):
    return pltpu.CompilerParams(
        dimension_semantics=("arbitrary",) * n_grid, vmem_limit_bytes=VMEM_LIMIT)


def _in_proj_kernel(x_ref, g_ref, w_ref, wff_ref, bff_ref,
                    fox_ref, diff_ref, chk_ref, cum_ref, carry_ref):
    tm = x_ref.shape[1]
    h = _rms(x_ref[0], g_ref[...]).astype(BF16)

    ff = _dot_nt(wff_ref[...], h) + bff_ref[...]
    log_f = jax.nn.log_sigmoid(ff)

    @pl.when(pl.program_id(1) == 0)
    def _():
        carry_ref[...] = jnp.zeros_like(carry_ref)

    n_grp = tm // LANES
    row = lax.broadcasted_iota(jnp.int32, (LANES, 2 * LANES), 0)
    col = lax.broadcasted_iota(jnp.int32, (LANES, 2 * LANES), 1)
    tri_ones = ((row <= col) | (col >= LANES)).astype(BF16)
    grouped = jnp.concatenate(
        [log_f[:, g * LANES:(g + 1) * LANES] for g in range(n_grp)], axis=0)
    r = _dot(jnp.concatenate(_split3(grouped), axis=0), tri_ones)
    r = r[:8 * n_grp] + r[8 * n_grp:16 * n_grp] + r[16 * n_grp:]
    run = carry_ref[...]
    pieces = []
    for g in range(n_grp):
        pieces.append(r[8 * g:8 * g + 8, :LANES] + run)
        run = run + r[8 * g:8 * g + 8, LANES:]
    cum_ref[0] = jnp.concatenate(pieces, axis=1)
    carry_ref[...] = run

    n_fox, n_diff = fox_ref.shape[2], diff_ref.shape[2]
    fox_ref[0] = _dot(h, w_ref[:, :n_fox]).astype(BF16)
    diff_ref[0] = _dot(h, w_ref[:, n_fox:n_fox + n_diff]).astype(BF16)
    chk_ref[0] = _dot(h, w_ref[:, n_fox + n_diff:]).astype(BF16)


def _in_proj(x, layer, g, w_main, wff_t, bff):
    b, s, d = x.shape
    n_main = w_main.shape[2]
    n_fox, n_diff, n_chk = 3 * FOX_W, 3 * DIFF_W, 3 * CHK_W
    assert n_main == n_fox + n_diff + n_chk
    grid = (b, s // TM_IN)
    row_spec = lambda n: pl.BlockSpec((1, TM_IN, n), lambda i, j: (i, j, 0))
    return pl.pallas_call(
        _in_proj_kernel,
        grid=grid,
        in_specs=[row_spec(d), _layer_spec((1, d), layer), _layer_spec((d, n_main), layer),
                  _layer_spec((8, d), layer), _layer_spec((8, 1), layer)],
        out_specs=[row_spec(n_fox), row_spec(n_diff), row_spec(n_chk),
                   pl.BlockSpec((1, 8, TM_IN), lambda i, j: (i, 0, j))],
        out_shape=[jax.ShapeDtypeStruct((b, s, n_fox), BF16),
                   jax.ShapeDtypeStruct((b, s, n_diff), BF16),
                   jax.ShapeDtypeStruct((b, s, n_chk), BF16),
                   jax.ShapeDtypeStruct((b, 8, s), F32)],
        scratch_shapes=[pltpu.VMEM((8, LANES), F32)],
        compiler_params=_params(2),
        name="in_proj",
    )(x, g, w_main, wff_t, bff)


def _half_masks():
    lane = lax.broadcasted_iota(jnp.int32, (1, LANES), 1)
    return lane < HEAD_DIM, lane >= HEAD_DIM


def _prep_streams(q_ref, k_ref, v_ref, qs_ref, kt_ref, vx_ref, v_masked):
    for grp in range(q_ref.shape[2] // LANES):
        lanes = slice(grp * LANES, (grp + 1) * LANES)
        kt_ref[grp] = k_ref[0, :, lanes].T
        q = q_ref[0, :, lanes] * jnp.asarray(HEAD_DIM ** -0.5, BF16)
        v = v_ref[0, :, lanes]
        for st, m in enumerate(_half_masks()):
            qs_ref[2 * grp + st] = jnp.where(m, q, jnp.zeros_like(q))
            vx_ref[2 * grp + st, :, :LANES] = (
                jnp.where(m, v, jnp.zeros_like(v)) if v_masked else v)
            vx_ref[2 * grp + st, :, LANES:] = jnp.ones_like(v)


def _fold_max(s):
    m = s[:, :LANES]
    for j in range(1, s.shape[1] // LANES):
        m = jnp.maximum(m, s[:, j * LANES:(j + 1) * LANES])
    return m


def _key_blocks(k_lo, k_hi, width):
    return [(k0, min(width, k_hi - k0)) for k0 in range(k_lo, k_hi, width)]


class _QueryTile:
    def __init__(self, grp, qb, blocks, bias_fn, tq=TQ):
        self.grp = grp
        self.tq = tq
        self.q0 = qb * tq
        self.blocks = blocks
        self.offs = [sum(n for _, n in blocks[:j]) for j in range(len(blocks))]
        self.bias_fn = bias_fn
        self.m = [None, None]
        self.mb = [None, None]
        self.acc = [None, None]


def _attend(tiles, qs_ref, kt_ref, vx_ref, s_ref, finish):
    for i, t in enumerate(tiles):
        t.slot = 2 * (i % 2)

    def pass_a(t, j):
        k0, n = t.blocks[j]
        kt = kt_ref[t.grp, :, k0:k0 + n]
        for st in range(2):
            s = (_dot(qs_ref[2 * t.grp + st, t.q0:t.q0 + t.tq, :], kt)
                 + t.bias_fn(st, k0, n))
            s_ref[t.slot + st, :, t.offs[j]:t.offs[j] + n] = s
            fm = _fold_max(s)
            t.m[st] = fm if t.m[st] is None else jnp.maximum(t.m[st], fm)

    def end_a(t):
        for st in range(2):
            t.mb[st] = jnp.broadcast_to(
                jnp.max(t.m[st], axis=-1, keepdims=True), (t.tq, LANES))

    def pass_b(t, j):
        k0, n = t.blocks[j]
        for st in range(2):
            s = s_ref[t.slot + st, :, t.offs[j]:t.offs[j] + n]
            e = jnp.exp(s - jnp.tile(t.mb[st], (1, n // LANES)))
            r = _dot(e.astype(BF16), vx_ref[2 * t.grp + st, k0:k0 + n, :])
            t.acc[st] = r if t.acc[st] is None else t.acc[st] + r

    for j in range(len(tiles[0].blocks)):
        pass_a(tiles[0], j)
    end_a(tiles[0])
    for i, t in enumerate(tiles):
        nxt = tiles[i + 1] if i + 1 < len(tiles) else None
        n_a = len(nxt.blocks) if nxt else 0
        for j in range(max(n_a, len(t.blocks))):
            if j < n_a:
                pass_a(nxt, j)
            if j < len(t.blocks):
                pass_b(t, j)
        if nxt:
            end_a(nxt)
        finish(t, t.acc)


def _attn_scratch(s, width, tq, s_cols):
    n_stream = 2 * (width // LANES)
    return [pltpu.VMEM((n_stream, s, LANES), BF16),
            pltpu.VMEM((width // LANES, LANES, s), BF16),
            pltpu.VMEM((n_stream, s, 2 * LANES), BF16),
            pltpu.VMEM((4, tq, s_cols), F32)]


def _qkv_specs(s, width):
    return [pl.BlockSpec((1, s, width), lambda i, c=c: (i, 0, c)) for c in range(3)]


def _fox_kernel(q_ref, k_ref, v_ref, cum_ref, o_ref, qs_ref, kt_ref, vx_ref, s_ref):
    nq = q_ref.shape[1] // TQ
    _prep_streams(q_ref, k_ref, v_ref, qs_ref, kt_ref, vx_ref, v_masked=True)
    ri = lax.broadcasted_iota(jnp.int32, (TQ, TQ), 0)
    ci = lax.broadcasted_iota(jnp.int32, (TQ, TQ), 1)
    causal_bias = jnp.where(ci <= ri, 0.0, NEG_INF).astype(F32)
    neg_cum = [-cum_ref[0, h:h + 1, :] for h in range(N_HEADS)]

    def bias_fn(grp, q0, st, k0, n):
        b = neg_cum[2 * grp + st][:, k0:k0 + n]
        if k0 + n <= q0:
            return b
        pad = [jnp.zeros((TQ, q0 - k0), F32)] if k0 < q0 else []
        return b + jnp.concatenate(pad + [causal_bias], axis=1)

    def finish(t, r):
        out = r[0][:, :LANES] / r[0][:, LANES:] + r[1][:, :LANES] / r[1][:, LANES:]
        o_ref[0, t.q0:t.q0 + TQ, t.grp * LANES:(t.grp + 1) * LANES] = out.astype(BF16)

    tiles = [_QueryTile(grp, qb, _key_blocks(0, (qb + 1) * TQ, TK),
                        functools.partial(bias_fn, grp, qb * TQ))
             for grp in range(FOX_W // LANES) for qb in range(nq)]
    _attend(tiles, qs_ref, kt_ref, vx_ref, s_ref, finish)


def _fox_attention(fox, cum):
    b, s, _ = fox.shape
    return pl.pallas_call(
        _fox_kernel,
        grid=(b,),
        in_specs=_qkv_specs(s, FOX_W) + [pl.BlockSpec((1, 8, s), lambda i: (i, 0, 0))],
        out_specs=pl.BlockSpec((1, s, FOX_W), lambda i: (i, 0, 0)),
        out_shape=jax.ShapeDtypeStruct((b, s, FOX_W), BF16),
        scratch_shapes=_attn_scratch(s, FOX_W, TQ, s),
        compiler_params=_params(1),
        name="fox_attn",
    )(fox, fox, fox, cum)


def _diff_kernel(q_ref, k_ref, v_ref, slope_ref, lam_ref, linit_ref, g_ref, o_ref,
                 qs_ref, kt_ref, vx_ref, s_ref):
    nq = q_ref.shape[1] // TQ
    _prep_streams(q_ref, k_ref, v_ref, qs_ref, kt_ref, vx_ref, v_masked=False)
    ri = lax.broadcasted_iota(jnp.int32, (TQ, TQ), 0)
    ci = lax.broadcasted_iota(jnp.int32, (TQ, TQ), 1)
    allowed = (ci // CHUNK) <= (ri // CHUNK)
    rel = jnp.where(ci <= ri, ci, 2 * ri - ci).astype(F32)
    lane_k = lax.broadcasted_iota(jnp.int32, (1, TK), 1)
    slope_t, diag_bias = [], []
    for h in range(N_HEADS_DIFF):
        slope_t.append(jnp.tile(slope_ref[h], (1, TK // LANES)))
        diag_bias.append(jnp.where(allowed, slope_t[h][:, :TQ] * rel, NEG_INF))
    lv = lam_ref[...]
    lam = (jnp.exp(jnp.sum(lv[0:1] * lv[1:2], axis=1, keepdims=True))
           - jnp.exp(jnp.sum(lv[2:3] * lv[3:4], axis=1, keepdims=True))
           + linit_ref[...])
    post = g_ref[...] * (1.0 - linit_ref[...])

    def bias_fn(grp, q0, st, k0, n):
        n_past = min(n, q0 - k0)
        parts = []
        if n_past > 0:
            parts.append(jnp.broadcast_to(
                slope_t[grp][:, :n_past] * (lane_k[:, :n_past] + (k0 - q0)).astype(F32),
                (TQ if n_past < n else 1, n_past)))
        if n_past < n:
            parts.append(diag_bias[grp])
        return parts[0] if len(parts) == 1 else jnp.concatenate(parts, axis=1)

    def finish(t, r):
        o = (r[0][:, :LANES] / r[0][:, LANES:]
             - lam * (r[1][:, :LANES] / r[1][:, LANES:]))
        o = o * lax.rsqrt(jnp.mean(o * o, axis=-1, keepdims=True) + RMS_EPS) * post
        o_ref[0, t.q0:t.q0 + TQ, t.grp * LANES:(t.grp + 1) * LANES] = o.astype(BF16)

    tiles = [_QueryTile(grp, qb, _key_blocks(0, (qb + 1) * TQ, TK),
                        functools.partial(bias_fn, grp, qb * TQ))
             for grp in range(N_HEADS_DIFF) for qb in range(nq)]
    _attend(tiles, qs_ref, kt_ref, vx_ref, s_ref, finish)


def _diff_attention(diff, layer, slopes, lam_vecs, linit, g_diff):
    b, s, _ = diff.shape
    return pl.pallas_call(
        _diff_kernel,
        grid=(b,),
        in_specs=_qkv_specs(s, DIFF_W) + [
            _const_spec((N_HEADS_DIFF, 1, LANES)),
            _layer_spec((4, HEAD_DIM), layer), _layer_spec((1, LANES), layer),
            _layer_spec((1, LANES), layer)],
        out_specs=pl.BlockSpec((1, s, DIFF_W), lambda i: (i, 0, 0)),
        out_shape=jax.ShapeDtypeStruct((b, s, DIFF_W), BF16),
        scratch_shapes=_attn_scratch(s, DIFF_W, TQ, s),
        compiler_params=_params(1),
        name="diff_attn",
    )(diff, diff, diff, slopes, lam_vecs, linit, g_diff)


def _chk_kernel(q_ref, k_ref, v_ref, rel_ref, o_ref, qs_ref, kt_ref, vx_ref, s_ref,
                bias_ref):
    nq = q_ref.shape[1] // TQ_CHK

    @pl.when(pl.program_id(0) == 0)
    def _():
        u = lax.broadcasted_iota(jnp.int32, (REL_PAD, ROLL_W), 1)
        r = lax.broadcasted_iota(jnp.int32, (REL_PAD, ROLL_W), 0)
        u = jnp.where(u < CHK_WIN, u, u - ROLL_W)
        idx = jnp.clip(CHK_BAND - u, -(CHUNK - 1), MAX_REL_DIST) + (CHUNK - 1)
        onehot = (r == idx).astype(BF16)
        g_all = _dot3(rel_ref[...], onehot)
        ri = lax.broadcasted_iota(jnp.int32, (TQ_CHK, CHK_WIN), 0) // CHUNK
        ci = lax.broadcasted_iota(jnp.int32, (TQ_CHK, CHK_WIN), 1) // CHUNK
        band = (ci >= ri) & (ci <= ri + CHUNK_LOOKBACK)
        for h in range(N_HEADS):
            g = jnp.broadcast_to(g_all[h:h + 1], (TQ_CHK, ROLL_W))
            toep = pltpu.roll(g, 0, 1, stride=1, stride_axis=0)
            bias_ref[h] = jnp.where(band, toep[:, :CHK_WIN], NEG_INF)

    _prep_streams(q_ref, k_ref, v_ref, qs_ref, kt_ref, vx_ref, v_masked=True)

    def bias_fn(grp, q0, st, k0, n):
        w0 = k0 - (q0 - CHK_BAND)
        return bias_ref[2 * grp + st, :, w0:w0 + n]

    def finish(t, r):
        out = r[0][:, :LANES] / r[0][:, LANES:] + r[1][:, :LANES] / r[1][:, LANES:]
        o_ref[0, t.q0:t.q0 + TQ_CHK, t.grp * LANES:(t.grp + 1) * LANES] = out.astype(BF16)

    tiles = [_QueryTile(grp, qb,
                        _key_blocks(max(0, qb * TQ_CHK - CHK_BAND), (qb + 1) * TQ_CHK, CHK_WIN),
                        functools.partial(bias_fn, grp, qb * TQ_CHK), TQ_CHK)
             for grp in range(CHK_W // LANES) for qb in range(nq)]
    _attend(tiles, qs_ref, kt_ref, vx_ref, s_ref, finish)


def _chk_attention(chk, layer, rel_pad):
    b, s, _ = chk.shape
    return pl.pallas_call(
        _chk_kernel,
        grid=(b,),
        in_specs=_qkv_specs(s, CHK_W) + [_layer_spec((8, REL_PAD), layer)],
        out_specs=pl.BlockSpec((1, s, CHK_W), lambda i: (i, 0, 0)),
        out_shape=jax.ShapeDtypeStruct((b, s, CHK_W), BF16),
        scratch_shapes=_attn_scratch(s, CHK_W, TQ_CHK, CHK_WIN) + [
            pltpu.VMEM((N_HEADS, TQ_CHK, CHK_WIN), F32)],
        compiler_params=_params(1),
        name="chunk_attn",
    )(chk, chk, chk, rel_pad)


def _ffn_kernel(x_ref, fox_ref, diff_ref, chk_ref, wo_ref, g_ref, wup_ref, cw_ref, cb_ref,
                wdn_ref, gfin_ref, o_ref, carry_ref, act_ref, t_ref, *, final_norm):
    tm, d_model = x_ref.shape[1], x_ref.shape[2]
    n_grp = tm // ROW_GROUP

    def transpose8(v, row0=0):
        n = v.shape[0]
        for j in range(d_model // LANES):
            t_ref[j, row0:row0 + n] = v[:, j * LANES:(j + 1) * LANES]
        cols = []
        for j in range(d_model // LANES):
            cols.append(jnp.concatenate(
                [t_ref[j, pl.ds(row0 + ROW_GROUP * grp + a, 8, stride=8), :]
                 for grp in range(n // ROW_GROUP) for a in range(8)], axis=0))
        return jnp.concatenate(cols, axis=1)

    x1_parts, h_parts = [], []
    for r0 in range(0, tm, tm // 2):
        rows = slice(r0, r0 + tm // 2)
        mix = jnp.concatenate([fox_ref[0, rows], diff_ref[0, rows], chk_ref[0, rows]], axis=1)
        x1_parts.append(transpose8(x_ref[0, rows] + _dot(mix, wo_ref[...]), r0))
        h_parts.append(_rms(x1_parts[-1], g_ref[...]).astype(BF16))
    x1 = jnp.concatenate(x1_parts, axis=0)
    h = jnp.concatenate(h_parts, axis=0)

    @pl.when(pl.program_id(1) == 0)
    def _():
        carry_ref[...] = jnp.zeros_like(carry_ref)

    sub = lax.broadcasted_iota(jnp.int32, (8, FF_CHUNK), 0)

    def conv_inputs(d, prev):
        r6, r7 = pltpu.roll(prev[:8], 1, 0), pltpu.roll(prev[8:], 1, 0)
        s1, s2 = [], []
        for grp in range(n_grp):
            g0 = ROW_GROUP * grp
            n6, n7 = pltpu.roll(d[g0 + 48:g0 + 56], 1, 0), pltpu.roll(d[g0 + 56:g0 + 64], 1, 0)
            h6, h7 = jnp.where(sub == 0, r6, n6), jnp.where(sub == 0, r7, n7)
            s1 += [h7, d[g0:g0 + 56]]
            s2 += [h6, h7, d[g0:g0 + 48]]
            r6, r7 = n6, n7
        return jnp.concatenate(s1, axis=0), jnp.concatenate(s2, axis=0)

    for c in range(D_FF // FF_CHUNK):
        ys = []
        for part in range(2):
            c0 = part * D_FF + c * FF_CHUNK
            cols = slice(c0, c0 + FF_CHUNK)
            if c == 0:
                d = jnp.concatenate([_dot(hp, wup_ref[:, cols]) for hp in h_parts], axis=0)
            else:
                d = _dot(h, wup_ref[:, cols])
            d1, d2 = conv_inputs(d, carry_ref[:, cols])
            carry_ref[:, cols] = d[tm - 16:]
            ys.append(cb_ref[:, cols] + cw_ref[2:3, cols] * d
                      + cw_ref[1:2, cols] * d1 + cw_ref[0:1, cols] * d2)
        hg = 0.5 * ys[1]
        act_ref[:, c * FF_CHUNK:(c + 1) * FF_CHUNK] = (
            (hg + hg * jnp.tanh(hg)) * ys[0]).astype(BF16)
    acc = x1 + _dot(act_ref[...], wdn_ref[...])
    if final_norm:
        acc = _rms(acc, gfin_ref[...])
    o_ref[0] = transpose8(acc)


def _out_ffn(x, fox_o, diff_o, chk_o, layer, w_out, g_ffn, w_up, conv_w, conv_b, w_down,
             g_final, final_norm):
    b, s, d = x.shape
    grid = (b, s // TM_FFN)
    row_spec = lambda n: pl.BlockSpec((1, TM_FFN, n), lambda i, j: (i, j, 0))
    single = lambda shape: _layer_spec(shape, layer, single_buffer=True)
    return pl.pallas_call(
        functools.partial(_ffn_kernel, final_norm=final_norm),
        grid=grid,
        in_specs=[row_spec(d), row_spec(FOX_W), row_spec(DIFF_W), row_spec(CHK_W),
                  single((FOX_W + DIFF_W + CHK_W, d)), _layer_spec((1, d), layer),
                  single((d, 2 * D_FF)),
                  _layer_spec((3, 2 * D_FF), layer), _layer_spec((1, 2 * D_FF), layer),
                  single((D_FF, d)), _const_spec((1, d))],
        out_specs=row_spec(d),
        out_shape=jax.ShapeDtypeStruct((b, s, d), F32),
        scratch_shapes=[pltpu.VMEM((16, 2 * D_FF), F32),
                        pltpu.VMEM((TM_FFN, D_FF), BF16),
                        pltpu.VMEM((d // LANES, TM_FFN, LANES), F32)],
        compiler_params=_params(2),
        name="out_ffn",
    )(x, fox_o, diff_o, chk_o, w_out, g_ffn, w_up, conv_w, conv_b, w_down, g_final)


def kernel(x, g_mix, w_in, b_fox_f, diff_lambda, g_diff, rel_bias, w_out,
           g_ffn, w_ffn_in, conv_w, conv_b, w_ffn_out, g_final):
    depth = w_in.shape[0]
    d = x.shape[-1]
    o_ff = 3 * FOX_W
    o_rest = o_ff + N_HEADS
    slopes = jnp.asarray([2.0 ** (-8.0 * (i + 1) / N_HEADS_DIFF)
                          for i in range(N_HEADS_DIFF)], F32)
    slopes = jnp.broadcast_to(slopes[:, None, None], (N_HEADS_DIFF, 1, LANES))
    w_main = jnp.concatenate([w_in[:, :, :o_ff], w_in[:, :, o_rest:]], axis=2).astype(BF16)
    wff_t = jnp.zeros((depth, 8, d), BF16).at[:, :N_HEADS].set(
        jnp.swapaxes(w_in[:, :, o_ff:o_rest], 1, 2).astype(BF16))
    bff = jnp.zeros((depth, 8, 1), F32).at[:, :N_HEADS, 0].set(b_fox_f.astype(F32))
    linit = jnp.stack([jnp.full((1, LANES), 0.8 - 0.6 * math.exp(-0.3 * l), F32)
                       for l in range(depth)])
    rel_pad = jnp.zeros((depth, 8, REL_PAD), F32).at[:, :N_HEADS, :rel_bias.shape[2]].set(
        rel_bias.astype(F32))
    w_out_b, w_up_b, w_down_b = (w.astype(BF16) for w in (w_out, w_ffn_in, w_ffn_out))

    for l in range(depth):
        fox, diff, chk, cum = _in_proj(x, l, g_mix[:, None], w_main, wff_t, bff)
        fox_o = _fox_attention(fox, cum)
        diff_o = _diff_attention(diff, l, slopes, diff_lambda.astype(F32), linit,
                                 g_diff[:, None].astype(F32))
        chk_o = _chk_attention(chk, l, rel_pad)
        x = _out_ffn(x, fox_o, diff_o, chk_o, l, w_out_b, g_ffn[:, None], w_up_b, conv_w,
                     conv_b[:, None], w_down_b, g_final[None], final_norm=(l == depth - 1))
    return x
```

```python
import functools
import math

import jax
import jax.numpy as jnp
from jax import lax
from jax.experimental import pallas as pl
from jax.experimental.pallas import tpu as pltpu

F32 = jnp.float32
BF16 = jnp.bfloat16

D_MODEL = 1024
HEAD_DIM = 64
CHUNK = 64
CHUNK_LOOKBACK = 8
MAX_REL_DIST = 128
D_FF = 2816
RMS_EPS = 1e-6
NEG_INF = -1e30
N_HEADS = 4
N_HEADS_DIFF = 2
FOX_W = 256
DIFF_W = 256
CHK_W = 256
LANES = 128
REL_PAD = 256

TM_IN = 1024
TM_FFN = 1024
FF_CHUNK = 256
ROW_GROUP = 64
TQ = 256
TK = 512
CHK_BAND = CHUNK_LOOKBACK * CHUNK
TQ_CHK = 128
CHK_WIN = CHK_BAND + TQ_CHK
ROLL_W = 1024
VMEM_LIMIT = 56 * 1024 * 1024


def _dot(a, b):
    return jnp.dot(a, b, preferred_element_type=F32)


def _dot_nt(a, b):
    return lax.dot_general(a, b, (((1,), (1,)), ((), ())), preferred_element_type=F32)


def _split3(x):
    hi = x.astype(BF16)
    r1 = x - hi.astype(F32)
    mid = r1.astype(BF16)
    lo = (r1 - mid.astype(F32)).astype(BF16)
    return hi, mid, lo


def _dot3(x, t):
    hi, mid, lo = _split3(x)
    return _dot(hi, t) + _dot(mid, t) + _dot(lo, t)


def _rms(x, g):
    return x * lax.rsqrt(jnp.mean(x * x, axis=-1, keepdims=True) + RMS_EPS) * g


def _const_spec(shape):
    nd = len(shape)
    return pl.BlockSpec(shape, lambda *_: (0,) * nd)


def _layer_spec(shape, layer, single_buffer=False):
    nd = len(shape)
    return pl.BlockSpec((None,) + tuple(shape), lambda *_: (layer,) + (0,) * nd,
                        pipeline_mode=pl.Buffered(1) if single_buffer else None)


def _params(n_grid):
    return pltpu.CompilerParams(
        dimension_semantics=("arbitrary",) * n_grid, vmem_limit_bytes=VMEM_LIMIT)


def _in_proj_kernel(x_ref, g_ref, w_ref, wff_ref, bff_ref,
                    *out_and_scratch):
    *qkv_refs, cum_ref, carry_ref = out_and_scratch
    tm = x_ref.shape[1]
    h = _rms(x_ref[0], g_ref[...]).astype(BF16)

    ff = _dot_nt(wff_ref[...], h) + bff_ref[...]
    log_f = jax.nn.log_sigmoid(ff)

    @pl.when(pl.program_id(1) == 0)
    def _():
        carry_ref[...] = jnp.zeros_like(carry_ref)

    n_grp = tm // LANES
    row = lax.broadcasted_iota(jnp.int32, (LANES, 2 * LANES), 0)
    col = lax.broadcasted_iota(jnp.int32, (LANES, 2 * LANES), 1)
    tri_ones = ((row <= col) | (col >= LANES)).astype(BF16)
    grouped = jnp.concatenate(
        [log_f[:, g * LANES:(g + 1) * LANES] for g in range(n_grp)], axis=0)
    r = _dot(jnp.concatenate(_split3(grouped), axis=0), tri_ones)
    r = r[:8 * n_grp] + r[8 * n_grp:16 * n_grp] + r[16 * n_grp:]
    run = carry_ref[...]
    pieces = []
    for g in range(n_grp):
        pieces.append(r[8 * g:8 * g + 8, :LANES] + run)
        run = run + r[8 * g:8 * g + 8, LANES:]
    cum_ref[0] = jnp.concatenate(pieces, axis=1)
    carry_ref[...] = run

    c0 = 0
    for o_ref in qkv_refs:
        n = o_ref.shape[2]
        o_ref[0] = _dot(h, w_ref[:, c0:c0 + n]).astype(BF16)
        c0 += n


def _in_proj(x, layer, g, w_main, wff_t, bff):
    b, s, d = x.shape
    n_main = w_main.shape[2]
    widths = [FOX_W] * 3 + [DIFF_W] * 3 + [CHK_W] * 3
    assert n_main == sum(widths)
    grid = (b, s // TM_IN)
    row_spec = lambda n: pl.BlockSpec((1, TM_IN, n), lambda i, j: (i, j, 0))
    return pl.pallas_call(
        _in_proj_kernel,
        grid=grid,
        in_specs=[row_spec(d), _layer_spec((1, d), layer), _layer_spec((d, n_main), layer),
                  _layer_spec((8, d), layer), _layer_spec((8, 1), layer)],
        out_specs=[row_spec(n) for n in widths] + [
            pl.BlockSpec((1, 8, TM_IN), lambda i, j: (i, 0, j))],
        out_shape=[jax.ShapeDtypeStruct((b, s, n), BF16) for n in widths] + [
            jax.ShapeDtypeStruct((b, 8, s), F32)],
        scratch_shapes=[pltpu.VMEM((8, LANES), F32)],
        compiler_params=_params(2),
        name="in_proj",
    )(x, g, w_main, wff_t, bff)


def _half_masks():
    lane = lax.broadcasted_iota(jnp.int32, (1, LANES), 1)
    return lane < HEAD_DIM, lane >= HEAD_DIM


def _prep_streams(q_ref, k_ref, v_ref, qs_ref, kt_ref, vx_ref, v_masked):
    for grp in range(q_ref.shape[2] // LANES):
        lanes = slice(grp * LANES, (grp + 1) * LANES)
        kt_ref[grp] = k_ref[0, :, lanes].T
        q = q_ref[0, :, lanes] * jnp.asarray(HEAD_DIM ** -0.5, BF16)
        v = v_ref[0, :, lanes]
        for st, m in enumerate(_half_masks()):
            qs_ref[2 * grp + st] = jnp.where(m, q, jnp.zeros_like(q))
            vx_ref[2 * grp + st, :, :LANES] = (
                jnp.where(m, v, jnp.zeros_like(v)) if v_masked else v)
            vx_ref[2 * grp + st, :, LANES:] = jnp.ones_like(v)


def _fold_max(s):
    m = s[:, :LANES]
    for j in range(1, s.shape[1] // LANES):
        m = jnp.maximum(m, s[:, j * LANES:(j + 1) * LANES])
    return m


def _key_blocks(k_lo, k_hi, width):
    return [(k0, min(width, k_hi - k0)) for k0 in range(k_lo, k_hi, width)]


class _QueryTile:
    def __init__(self, grp, qb, blocks, bias_fn, tq=TQ):
        self.grp = grp
        self.tq = tq
        self.q0 = qb * tq
        self.blocks = blocks
        self.offs = [sum(n for _, n in blocks[:j]) for j in range(len(blocks))]
        self.bias_fn = bias_fn
        self.m = [None, None]
        self.mb = [None, None]
        self.acc = [None, None]


def _attend(tiles, qs_ref, kt_ref, vx_ref, s_ref, finish):
    for i, t in enumerate(tiles):
        t.slot = 2 * (i % 2)

    def pass_a(t, j):
        k0, n = t.blocks[j]
        kt = kt_ref[t.grp, :, k0:k0 + n]
        for st in range(2):
            s = (_dot(qs_ref[2 * t.grp + st, t.q0:t.q0 + t.tq, :], kt)
                 + t.bias_fn(st, k0, n))
            s_ref[t.slot + st, :, t.offs[j]:t.offs[j] + n] = s
            fm = _fold_max(s)
            t.m[st] = fm if t.m[st] is None else jnp.maximum(t.m[st], fm)

    def end_a(t):
        for st in range(2):
            t.mb[st] = jnp.broadcast_to(
                jnp.max(t.m[st], axis=-1, keepdims=True), (t.tq, LANES))

    def pass_b(t, j):
        k0, n = t.blocks[j]
        for st in range(2):
            s = s_ref[t.slot + st, :, t.offs[j]:t.offs[j] + n]
            e = jnp.exp(s - jnp.tile(t.mb[st], (1, n // LANES)))
            r = _dot(e.astype(BF16), vx_ref[2 * t.grp + st, k0:k0 + n, :])
            t.acc[st] = r if t.acc[st] is None else t.acc[st] + r

    for j in range(len(tiles[0].blocks)):
        pass_a(tiles[0], j)
    end_a(tiles[0])
    for i, t in enumerate(tiles):
        nxt = tiles[i + 1] if i + 1 < len(tiles) else None
        n_a = len(nxt.blocks) if nxt else 0
        for j in range(max(n_a, len(t.blocks))):
            if j < n_a:
                pass_a(nxt, j)
            if j < len(t.blocks):
                pass_b(t, j)
        if nxt:
            end_a(nxt)
        finish(t, t.acc)


def _attn_scratch(s, width, tq, s_cols):
    n_stream = 2 * (width // LANES)
    return [pltpu.VMEM((n_stream, s, LANES), BF16),
            pltpu.VMEM((width // LANES, LANES, s), BF16),
            pltpu.VMEM((n_stream, s, 2 * LANES), BF16),
            pltpu.VMEM((4, tq, s_cols), F32)]


def _qkv_specs(s, width):
    return [pl.BlockSpec((1, s, width), lambda i: (i, 0, 0))] * 3


def _fox_kernel(q_ref, k_ref, v_ref, cum_ref, o_ref, qs_ref, kt_ref, vx_ref, s_ref):
    nq = q_ref.shape[1] // TQ
    _prep_streams(q_ref, k_ref, v_ref, qs_ref, kt_ref, vx_ref, v_masked=True)
    ri = lax.broadcasted_iota(jnp.int32, (TQ, TQ), 0)
    ci = lax.broadcasted_iota(jnp.int32, (TQ, TQ), 1)
    causal_bias = jnp.where(ci <= ri, 0.0, NEG_INF).astype(F32)
    neg_cum = [-cum_ref[0, h:h + 1, :] for h in range(N_HEADS)]

    def bias_fn(grp, q0, st, k0, n):
        b = neg_cum[2 * grp + st][:, k0:k0 + n]
        if k0 + n <= q0:
            return b
        pad = [jnp.zeros((TQ, q0 - k0), F32)] if k0 < q0 else []
        return b + jnp.concatenate(pad + [causal_bias], axis=1)

    def finish(t, r):
        out = r[0][:, :LANES] / r[0][:, LANES:] + r[1][:, :LANES] / r[1][:, LANES:]
        o_ref[0, t.q0:t.q0 + TQ, t.grp * LANES:(t.grp + 1) * LANES] = out.astype(BF16)

    tiles = [_QueryTile(grp, qb, _key_blocks(0, (qb + 1) * TQ, TK),
                        functools.partial(bias_fn, grp, qb * TQ))
             for grp in range(FOX_W // LANES) for qb in range(nq)]
    _attend(tiles, qs_ref, kt_ref, vx_ref, s_ref, finish)


def _fox_attention(q, k, v, cum):
    b, s, _ = q.shape
    return pl.pallas_call(
        _fox_kernel,
        grid=(b,),
        in_specs=_qkv_specs(s, FOX_W) + [pl.BlockSpec((1, 8, s), lambda i: (i, 0, 0))],
        out_specs=pl.BlockSpec((1, s, FOX_W), lambda i: (i, 0, 0)),
        out_shape=jax.ShapeDtypeStruct((b, s, FOX_W), BF16),
        scratch_shapes=_attn_scratch(s, FOX_W, TQ, s),
        compiler_params=_params(1),
        name="fox_attn",
    )(q, k, v, cum)


def _diff_kernel(q_ref, k_ref, v_ref, slope_ref, lam_ref, linit_ref, g_ref, o_ref,
                 qs_ref, kt_ref, vx_ref, s_ref):
    nq = q_ref.shape[1] // TQ
    _prep_streams(q_ref, k_ref, v_ref, qs_ref, kt_ref, vx_ref, v_masked=False)
    ri = lax.broadcasted_iota(jnp.int32, (TQ, TQ), 0)
    ci = lax.broadcasted_iota(jnp.int32, (TQ, TQ), 1)
    allowed = (ci // CHUNK) <= (ri // CHUNK)
    rel = jnp.where(ci <= ri, ci, 2 * ri - ci).astype(F32)
    lane_k = lax.broadcasted_iota(jnp.int32, (1, TK), 1)
    slope_t, diag_bias = [], []
    for h in range(N_HEADS_DIFF):
        slope_t.append(jnp.tile(slope_ref[h], (1, TK // LANES)))
        diag_bias.append(jnp.where(allowed, slope_t[h][:, :TQ] * rel, NEG_INF))
    lv = lam_ref[...]
    lam = (jnp.exp(jnp.sum(lv[0:1] * lv[1:2], axis=1, keepdims=True))
           - jnp.exp(jnp.sum(lv[2:3] * lv[3:4], axis=1, keepdims=True))
           + linit_ref[...])
    post = g_ref[...] * (1.0 - linit_ref[...])

    def bias_fn(grp, q0, st, k0, n):
        n_past = min(n, q0 - k0)
        parts = []
        if n_past > 0:
            parts.append(jnp.broadcast_to(
                slope_t[grp][:, :n_past] * (lane_k[:, :n_past] + (k0 - q0)).astype(F32),
                (TQ if n_past < n else 1, n_past)))
        if n_past < n:
            parts.append(diag_bias[grp])
        return parts[0] if len(parts) == 1 else jnp.concatenate(parts, axis=1)

    def finish(t, r):
        o = (r[0][:, :LANES] / r[0][:, LANES:]
             - lam * (r[1][:, :LANES] / r[1][:, LANES:]))
        o = o * lax.rsqrt(jnp.mean(o * o, axis=-1, keepdims=True) + RMS_EPS) * post
        o_ref[0, t.q0:t.q0 + TQ, t.grp * LANES:(t.grp + 1) * LANES] = o.astype(BF16)

    tiles = [_QueryTile(grp, qb, _key_blocks(0, (qb + 1) * TQ, TK),
                        functools.partial(bias_fn, grp, qb * TQ))
             for grp in range(N_HEADS_DIFF) for qb in range(nq)]
    _attend(tiles, qs_ref, kt_ref, vx_ref, s_ref, finish)


def _diff_attention(q, k, v, layer, slopes, lam_vecs, linit, g_diff):
    b, s, _ = q.shape
    return pl.pallas_call(
        _diff_kernel,
        grid=(b,),
        in_specs=_qkv_specs(s, DIFF_W) + [
            _const_spec((N_HEADS_DIFF, 1, LANES)),
            _layer_spec((4, HEAD_DIM), layer), _layer_spec((1, LANES), layer),
            _layer_spec((1, LANES), layer)],
        out_specs=pl.BlockSpec((1, s, DIFF_W), lambda i: (i, 0, 0)),
        out_shape=jax.ShapeDtypeStruct((b, s, DIFF_W), BF16),
        scratch_shapes=_attn_scratch(s, DIFF_W, TQ, s),
        compiler_params=_params(1),
        name="diff_attn",
    )(q, k, v, slopes, lam_vecs, linit, g_diff)


def _chk_kernel(q_ref, k_ref, v_ref, rel_ref, o_ref, qs_ref, kt_ref, vx_ref, s_ref,
                bias_ref):
    nq = q_ref.shape[1] // TQ_CHK

    @pl.when(pl.program_id(0) == 0)
    def _():
        u = lax.broadcasted_iota(jnp.int32, (REL_PAD, ROLL_W), 1)
        r = lax.broadcasted_iota(jnp.int32, (REL_PAD, ROLL_W), 0)
        u = jnp.where(u < CHK_WIN, u, u - ROLL_W)
        idx = jnp.clip(CHK_BAND - u, -(CHUNK - 1), MAX_REL_DIST) + (CHUNK - 1)
        onehot = (r == idx).astype(BF16)
        g_all = _dot3(rel_ref[...], onehot)
        ri = lax.broadcasted_iota(jnp.int32, (TQ_CHK, CHK_WIN), 0) // CHUNK
        ci = lax.broadcasted_iota(jnp.int32, (TQ_CHK, CHK_WIN), 1) // CHUNK
        band = (ci >= ri) & (ci <= ri + CHUNK_LOOKBACK)
        for h in range(N_HEADS):
            g = jnp.broadcast_to(g_all[h:h + 1], (TQ_CHK, ROLL_W))
            toep = pltpu.roll(g, 0, 1, stride=1, stride_axis=0)
            bias_ref[h] = jnp.where(band, toep[:, :CHK_WIN], NEG_INF)

    _prep_streams(q_ref, k_ref, v_ref, qs_ref, kt_ref, vx_ref, v_masked=True)

    def bias_fn(grp, q0, st, k0, n):
        w0 = k0 - (q0 - CHK_BAND)
        return bias_ref[2 * grp + st, :, w0:w0 + n]

    def finish(t, r):
        out = r[0][:, :LANES] / r[0][:, LANES:] + r[1][:, :LANES] / r[1][:, LANES:]
        o_ref[0, t.q0:t.q0 + TQ_CHK, t.grp * LANES:(t.grp + 1) * LANES] = out.astype(BF16)

    tiles = [_QueryTile(grp, qb,
                        _key_blocks(max(0, qb * TQ_CHK - CHK_BAND), (qb + 1) * TQ_CHK, CHK_WIN),
                        functools.partial(bias_fn, grp, qb * TQ_CHK), TQ_CHK)
             for grp in range(CHK_W // LANES) for qb in range(nq)]
    _attend(tiles, qs_ref, kt_ref, vx_ref, s_ref, finish)


def _chk_attention(q, k, v, layer, rel_pad):
    b, s, _ = q.shape
    return pl.pallas_call(
        _chk_kernel,
        grid=(b,),
        in_specs=_qkv_specs(s, CHK_W) + [_layer_spec((8, REL_PAD), layer)],
        out_specs=pl.BlockSpec((1, s, CHK_W), lambda i: (i, 0, 0)),
        out_shape=jax.ShapeDtypeStruct((b, s, CHK_W), BF16),
        scratch_shapes=_attn_scratch(s, CHK_W, TQ_CHK, CHK_WIN) + [
            pltpu.VMEM((N_HEADS, TQ_CHK, CHK_WIN), F32)],
        compiler_params=_params(1),
        name="chunk_attn",
    )(q, k, v, rel_pad)


def _ffn_kernel(x_ref, fox_ref, diff_ref, chk_ref, wo_ref, g_ref, wup_ref, cw_ref, cb_ref,
                wdn_ref, gfin_ref, o_ref, carry_ref, act_ref, t_ref, *, final_norm):
    tm, d_model = x_ref.shape[1], x_ref.shape[2]
    n_grp = tm // ROW_GROUP

    def transpose8(v, row0=0):
        n = v.shape[0]
        for j in range(d_model // LANES):
            t_ref[j, row0:row0 + n] = v[:, j * LANES:(j + 1) * LANES]
        cols = []
        for j in range(d_model // LANES):
            cols.append(jnp.concatenate(
                [t_ref[j, pl.ds(row0 + ROW_GROUP * grp + a, 8, stride=8), :]
                 for grp in range(n // ROW_GROUP) for a in range(8)], axis=0))
        return jnp.concatenate(cols, axis=1)

    x1_parts, h_parts = [], []
    for r0 in range(0, tm, tm // 2):
        rows = slice(r0, r0 + tm // 2)
        mix = jnp.concatenate([fox_ref[0, rows], diff_ref[0, rows], chk_ref[0, rows]], axis=1)
        x1_parts.append(transpose8(x_ref[0, rows] + _dot(mix, wo_ref[...]), r0))
        h_parts.append(_rms(x1_parts[-1], g_ref[...]).astype(BF16))
    x1 = jnp.concatenate(x1_parts, axis=0)
    h = jnp.concatenate(h_parts, axis=0)

    @pl.when(pl.program_id(1) == 0)
    def _():
        carry_ref[...] = jnp.zeros_like(carry_ref)

    sub = lax.broadcasted_iota(jnp.int32, (8, FF_CHUNK), 0)

    def conv_inputs(d, prev):
        r6, r7 = pltpu.roll(prev[:8], 1, 0), pltpu.roll(prev[8:], 1, 0)
        s1, s2 = [], []
        for grp in range(n_grp):
            g0 = ROW_GROUP * grp
            n6, n7 = pltpu.roll(d[g0 + 48:g0 + 56], 1, 0), pltpu.roll(d[g0 + 56:g0 + 64], 1, 0)
            h6, h7 = jnp.where(sub == 0, r6, n6), jnp.where(sub == 0, r7, n7)
            s1 += [h7, d[g0:g0 + 56]]
            s2 += [h6, h7, d[g0:g0 + 48]]
            r6, r7 = n6, n7
        return jnp.concatenate(s1, axis=0), jnp.concatenate(s2, axis=0)

    for c in range(D_FF // FF_CHUNK):
        ys = []
        for part in range(2):
            c0 = part * D_FF + c * FF_CHUNK
            cols = slice(c0, c0 + FF_CHUNK)
            if c == 0:
                d = jnp.concatenate([_dot(hp, wup_ref[:, cols]) for hp in h_parts], axis=0)
            else:
                d = _dot(h, wup_ref[:, cols])
            d1, d2 = conv_inputs(d, carry_ref[:, cols])
            carry_ref[:, cols] = d[tm - 16:]
            ys.append(cb_ref[:, cols] + cw_ref[2:3, cols] * d
                      + cw_ref[1:2, cols] * d1 + cw_ref[0:1, cols] * d2)
        hg = 0.5 * ys[1]
        act_ref[:, c * FF_CHUNK:(c + 1) * FF_CHUNK] = (
            (hg + hg * jnp.tanh(hg)) * ys[0]).astype(BF16)
    acc = x1 + _dot(act_ref[...], wdn_ref[...])
    if final_norm:
        acc = _rms(acc, gfin_ref[...])
    o_ref[0] = transpose8(acc)


def _out_ffn(x, fox_o, diff_o, chk_o, layer, w_out, g_ffn, w_up, conv_w, conv_b, w_down,
             g_final, final_norm):
    b, s, d = x.shape
    grid = (b, s // TM_FFN)
    row_spec = lambda n: pl.BlockSpec((1, TM_FFN, n), lambda i, j: (i, j, 0))
    single = lambda shape: _layer_spec(shape, layer, single_buffer=True)
    return pl.pallas_call(
        functools.partial(_ffn_kernel, final_norm=final_norm),
        grid=grid,
        in_specs=[row_spec(d), row_spec(FOX_W), row_spec(DIFF_W), row_spec(CHK_W),
                  single((FOX_W + DIFF_W + CHK_W, d)), _layer_spec((1, d), layer),
                  single((d, 2 * D_FF)),
                  _layer_spec((3, 2 * D_FF), layer), _layer_spec((1, 2 * D_FF), layer),
                  single((D_FF, d)), _const_spec((1, d))],
        out_specs=row_spec(d),
        out_shape=jax.ShapeDtypeStruct((b, s, d), F32),
        scratch_shapes=[pltpu.VMEM((16, 2 * D_FF), F32),
                        pltpu.VMEM((TM_FFN, D_FF), BF16),
                        pltpu.VMEM((d // LANES, TM_FFN, LANES), F32)],
        compiler_params=_params(2),
        name="out_ffn",
    )(x, fox_o, diff_o, chk_o, w_out, g_ffn, w_up, conv_w, conv_b, w_down, g_final)


def kernel(x, g_mix, w_in, b_fox_f, diff_lambda, g_diff, rel_bias, w_out,
           g_ffn, w_ffn_in, conv_w, conv_b, w_ffn_out, g_final):
    depth = w_in.shape[0]
    d = x.shape[-1]
    o_ff = 3 * FOX_W
    o_rest = o_ff + N_HEADS
    slopes = jnp.asarray([2.0 ** (-8.0 * (i + 1) / N_HEADS_DIFF)
                          for i in range(N_HEADS_DIFF)], F32)
    slopes = jnp.broadcast_to(slopes[:, None, None], (N_HEADS_DIFF, 1, LANES))
    w_main = jnp.concatenate([w_in[:, :, :o_ff], w_in[:, :, o_rest:]], axis=2).astype(BF16)
    wff_t = jnp.zeros((depth, 8, d), BF16).at[:, :N_HEADS].set(
        jnp.swapaxes(w_in[:, :, o_ff:o_rest], 1, 2).astype(BF16))
    bff = jnp.zeros((depth, 8, 1), F32).at[:, :N_HEADS, 0].set(b_fox_f.astype(F32))
    linit = jnp.stack([jnp.full((1, LANES), 0.8 - 0.6 * math.exp(-0.3 * l), F32)
                       for l in range(depth)])
    rel_pad = jnp.zeros((depth, 8, REL_PAD), F32).at[:, :N_HEADS, :rel_bias.shape[2]].set(
        rel_bias.astype(F32))
    w_out_b, w_up_b, w_down_b = (w.astype(BF16) for w in (w_out, w_ffn_in, w_ffn_out))

    for l in range(depth):
        *qkv, cum = _in_proj(x, l, g_mix[:, None], w_main, wff_t, bff)
        fox_o = _fox_attention(*qkv[0:3], cum)
        diff_o = _diff_attention(*qkv[3:6], l, slopes, diff_lambda.astype(F32), linit,
                                 g_diff[:, None].astype(F32))
        chk_o = _chk_attention(*qkv[6:9], l, rel_pad)
        x = _out_ffn(x, fox_o, diff_o, chk_o, l, w_out_b, g_ffn[:, None], w_up_b, conv_w,
                     conv_b[:, None], w_down_b, g_final[None], final_norm=(l == depth - 1))
    return x
```

```python
import functools
import math

import jax
import jax.numpy as jnp
from jax import lax
from jax.experimental import pallas as pl
from jax.experimental.pallas import tpu as pltpu

F32 = jnp.float32
BF16 = jnp.bfloat16

D_MODEL = 1024
HEAD_DIM = 64
CHUNK = 64
CHUNK_LOOKBACK = 8
MAX_REL_DIST = 128
D_FF = 2816
RMS_EPS = 1e-6
NEG_INF = -1e30
N_HEADS = 4
N_HEADS_DIFF = 2
FOX_W = 256
DIFF_W = 256
CHK_W = 256
LANES = 128
REL_PAD = 256

TM_IN = 1024
TM_FFN = 1024
FF_CHUNK = 256
FFN_SUBTILES = 2
TQ = 256
TK = 512
CHK_BAND = CHUNK_LOOKBACK * CHUNK
TQ_CHK = 128
CHK_WIN = CHK_BAND + TQ_CHK
ROLL_W = 1024
VMEM_LIMIT = 56 * 1024 * 1024


def _dot(a, b):
    return jnp.dot(a, b, preferred_element_type=F32)


def _dot_nt(a, b):
    return lax.dot_general(a, b, (((1,), (1,)), ((), ())), preferred_element_type=F32)


def _split3(x):
    hi = x.astype(BF16)
    r1 = x - hi.astype(F32)
    mid = r1.astype(BF16)
    lo = (r1 - mid.astype(F32)).astype(BF16)
    return hi, mid, lo


def _dot3(x, t):
    hi, mid, lo = _split3(x)
    return _dot(hi, t) + _dot(mid, t) + _dot(lo, t)


def _rms(x, g):
    return x * lax.rsqrt(jnp.mean(x * x, axis=-1, keepdims=True) + RMS_EPS) * g


def _const_spec(shape):
    nd = len(shape)
    return pl.BlockSpec(shape, lambda *_: (0,) * nd)


def _layer_spec(shape, layer, single_buffer=False):
    nd = len(shape)
    return pl.BlockSpec((None,) + tuple(shape), lambda *_: (layer,) + (0,) * nd,
                        pipeline_mode=pl.Buffered(1) if single_buffer else None)


def _params(n_grid):
    return pltpu.CompilerParams(
        dimension_semantics=("arbitrary",) * n_grid, vmem_limit_bytes=VMEM_LIMIT)


def _in_proj_kernel(x_ref, g_ref, w_ref, wff_ref, bff_ref,
                    *out_and_scratch):
    *qkv_refs, cum_ref, carry_ref = out_and_scratch
    tm = x_ref.shape[1]
    h = _rms(x_ref[0], g_ref[...]).astype(BF16)

    ff = _dot_nt(wff_ref[...], h) + bff_ref[...]
    log_f = jax.nn.log_sigmoid(ff)

    @pl.when(pl.program_id(1) == 0)
    def _():
        carry_ref[...] = jnp.zeros_like(carry_ref)

    n_grp = tm // LANES
    row = lax.broadcasted_iota(jnp.int32, (LANES, 2 * LANES), 0)
    col = lax.broadcasted_iota(jnp.int32, (LANES, 2 * LANES), 1)
    tri_ones = ((row <= col) | (col >= LANES)).astype(BF16)
    grouped = jnp.concatenate(
        [log_f[:, g * LANES:(g + 1) * LANES] for g in range(n_grp)], axis=0)
    r = _dot(jnp.concatenate(_split3(grouped), axis=0), tri_ones)
    r = r[:8 * n_grp] + r[8 * n_grp:16 * n_grp] + r[16 * n_grp:]
    run = carry_ref[...]
    pieces = []
    for g in range(n_grp):
        pieces.append(r[8 * g:8 * g + 8, :LANES] + run)
        run = run + r[8 * g:8 * g + 8, LANES:]
    cum_ref[0] = jnp.concatenate(pieces, axis=1)
    carry_ref[...] = run

    c0 = 0
    for o_ref in qkv_refs:
        n = o_ref.shape[2]
        o_ref[0] = _dot(h, w_ref[:, c0:c0 + n]).astype(BF16)
        c0 += n


def _in_proj(x, layer, g, w_main, wff_t, bff):
    b, s, d = x.shape
    n_main = w_main.shape[2]
    widths = [FOX_W] * 3 + [DIFF_W] * 3 + [CHK_W] * 3
    assert n_main == sum(widths)
    grid = (b, s // TM_IN)
    row_spec = lambda n: pl.BlockSpec((1, TM_IN, n), lambda i, j: (i, j, 0))
    return pl.pallas_call(
        _in_proj_kernel,
        grid=grid,
        in_specs=[row_spec(d), _layer_spec((1, d), layer), _layer_spec((d, n_main), layer),
                  _layer_spec((8, d), layer), _layer_spec((8, 1), layer)],
        out_specs=[row_spec(n) for n in widths] + [
            pl.BlockSpec((1, 8, TM_IN), lambda i, j: (i, 0, j))],
        out_shape=[jax.ShapeDtypeStruct((b, s, n), BF16) for n in widths] + [
            jax.ShapeDtypeStruct((b, 8, s), F32)],
        scratch_shapes=[pltpu.VMEM((8, LANES), F32)],
        compiler_params=_params(2),
        name="in_proj",
    )(x, g, w_main, wff_t, bff)


def _half_masks():
    lane = lax.broadcasted_iota(jnp.int32, (1, LANES), 1)
    return lane < HEAD_DIM, lane >= HEAD_DIM


def _prep_streams(q_ref, k_ref, v_ref, qs_ref, kt_ref, vx_ref, v_masked):
    for grp in range(q_ref.shape[2] // LANES):
        lanes = slice(grp * LANES, (grp + 1) * LANES)
        kt_ref[grp] = k_ref[0, :, lanes].T
        q = q_ref[0, :, lanes] * jnp.asarray(HEAD_DIM ** -0.5, BF16)
        v = v_ref[0, :, lanes]
        for st, m in enumerate(_half_masks()):
            qs_ref[2 * grp + st] = jnp.where(m, q, jnp.zeros_like(q))
            vx_ref[2 * grp + st, :, :LANES] = (
                jnp.where(m, v, jnp.zeros_like(v)) if v_masked else v)
            vx_ref[2 * grp + st, :, LANES:] = jnp.ones_like(v)


def _fold_max(s):
    m = s[:, :LANES]
    for j in range(1, s.shape[1] // LANES):
        m = jnp.maximum(m, s[:, j * LANES:(j + 1) * LANES])
    return m


def _key_blocks(k_lo, k_hi, width):
    return [(k0, min(width, k_hi - k0)) for k0 in range(k_lo, k_hi, width)]


class _QueryTile:
    def __init__(self, grp, qb, blocks, bias_fn, tq=TQ):
        self.grp = grp
        self.tq = tq
        self.q0 = qb * tq
        self.blocks = blocks
        self.offs = [sum(n for _, n in blocks[:j]) for j in range(len(blocks))]
        self.bias_fn = bias_fn
        self.m = [None, None]
        self.mb = [None, None]
        self.acc = [None, None]


def _attend(tiles, qs_ref, kt_ref, vx_ref, s_ref, finish):
    for i, t in enumerate(tiles):
        t.slot = 2 * (i % 2)

    def pass_a(t, j):
        k0, n = t.blocks[j]
        kt = kt_ref[t.grp, :, k0:k0 + n]
        for st in range(2):
            s = (_dot(qs_ref[2 * t.grp + st, t.q0:t.q0 + t.tq, :], kt)
                 + t.bias_fn(st, k0, n))
            s_ref[t.slot + st, :, t.offs[j]:t.offs[j] + n] = s
            fm = _fold_max(s)
            t.m[st] = fm if t.m[st] is None else jnp.maximum(t.m[st], fm)

    def end_a(t):
        for st in range(2):
            t.mb[st] = jnp.broadcast_to(
                jnp.max(t.m[st], axis=-1, keepdims=True), (t.tq, LANES))

    def pass_b(t, j):
        k0, n = t.blocks[j]
        for st in range(2):
            s = s_ref[t.slot + st, :, t.offs[j]:t.offs[j] + n]
            e = jnp.exp(s - jnp.tile(t.mb[st], (1, n // LANES)))
            r = _dot(e.astype(BF16), vx_ref[2 * t.grp + st, k0:k0 + n, :])
            t.acc[st] = r if t.acc[st] is None else t.acc[st] + r

    for j in range(len(tiles[0].blocks)):
        pass_a(tiles[0], j)
    end_a(tiles[0])
    for i, t in enumerate(tiles):
        nxt = tiles[i + 1] if i + 1 < len(tiles) else None
        n_a = len(nxt.blocks) if nxt else 0
        for j in range(max(n_a, len(t.blocks))):
            if j < n_a:
                pass_a(nxt, j)
            if j < len(t.blocks):
                pass_b(t, j)
        if nxt:
            end_a(nxt)
        finish(t, t.acc)


def _attn_scratch(s, width, tq, s_cols):
    n_stream = 2 * (width // LANES)
    return [pltpu.VMEM((n_stream, s, LANES), BF16),
            pltpu.VMEM((width // LANES, LANES, s), BF16),
            pltpu.VMEM((n_stream, s, 2 * LANES), BF16),
            pltpu.VMEM((4, tq, s_cols), F32)]


def _qkv_specs(s, width):
    return [pl.BlockSpec((1, s, width), lambda i: (i, 0, 0))] * 3


def _fox_kernel(q_ref, k_ref, v_ref, cum_ref, o_ref, qs_ref, kt_ref, vx_ref, s_ref):
    nq = q_ref.shape[1] // TQ
    _prep_streams(q_ref, k_ref, v_ref, qs_ref, kt_ref, vx_ref, v_masked=True)
    ri = lax.broadcasted_iota(jnp.int32, (TQ, TQ), 0)
    ci = lax.broadcasted_iota(jnp.int32, (TQ, TQ), 1)
    causal_bias = jnp.where(ci <= ri, 0.0, NEG_INF).astype(F32)
    neg_cum = [-cum_ref[0, h:h + 1, :] for h in range(N_HEADS)]

    def bias_fn(grp, q0, st, k0, n):
        b = neg_cum[2 * grp + st][:, k0:k0 + n]
        if k0 + n <= q0:
            return b
        pad = [jnp.zeros((TQ, q0 - k0), F32)] if k0 < q0 else []
        return b + jnp.concatenate(pad + [causal_bias], axis=1)

    def finish(t, r):
        out = r[0][:, :LANES] / r[0][:, LANES:] + r[1][:, :LANES] / r[1][:, LANES:]
        o_ref[0, t.q0:t.q0 + TQ, t.grp * LANES:(t.grp + 1) * LANES] = out.astype(BF16)

    tiles = [_QueryTile(grp, qb, _key_blocks(0, (qb + 1) * TQ, TK),
                        functools.partial(bias_fn, grp, qb * TQ))
             for grp in range(FOX_W // LANES) for qb in range(nq)]
    _attend(tiles, qs_ref, kt_ref, vx_ref, s_ref, finish)


def _fox_attention(q, k, v, cum):
    b, s, _ = q.shape
    return pl.pallas_call(
        _fox_kernel,
        grid=(b,),
        in_specs=_qkv_specs(s, FOX_W) + [pl.BlockSpec((1, 8, s), lambda i: (i, 0, 0))],
        out_specs=pl.BlockSpec((1, s, FOX_W), lambda i: (i, 0, 0)),
        out_shape=jax.ShapeDtypeStruct((b, s, FOX_W), BF16),
        scratch_shapes=_attn_scratch(s, FOX_W, TQ, s),
        compiler_params=_params(1),
        name="fox_attn",
    )(q, k, v, cum)


def _diff_kernel(q_ref, k_ref, v_ref, slope_ref, lam_ref, linit_ref, g_ref, o_ref,
                 qs_ref, kt_ref, vx_ref, s_ref):
    nq = q_ref.shape[1] // TQ
    _prep_streams(q_ref, k_ref, v_ref, qs_ref, kt_ref, vx_ref, v_masked=False)
    ri = lax.broadcasted_iota(jnp.int32, (TQ, TQ), 0)
    ci = lax.broadcasted_iota(jnp.int32, (TQ, TQ), 1)
    allowed = (ci // CHUNK) <= (ri // CHUNK)
    rel = jnp.where(ci <= ri, ci, 2 * ri - ci).astype(F32)
    lane_k = lax.broadcasted_iota(jnp.int32, (1, TK), 1)
    slope_t, diag_bias = [], []
    for h in range(N_HEADS_DIFF):
        slope_t.append(jnp.tile(slope_ref[h], (1, TK // LANES)))
        diag_bias.append(jnp.where(allowed, slope_t[h][:, :TQ] * rel, NEG_INF))
    lv = lam_ref[...]
    lam = (jnp.exp(jnp.sum(lv[0:1] * lv[1:2], axis=1, keepdims=True))
           - jnp.exp(jnp.sum(lv[2:3] * lv[3:4], axis=1, keepdims=True))
           + linit_ref[...])
    post = g_ref[...] * (1.0 - linit_ref[...])

    def bias_fn(grp, q0, st, k0, n):
        n_past = min(n, q0 - k0)
        parts = []
        if n_past > 0:
            parts.append(jnp.broadcast_to(
                slope_t[grp][:, :n_past] * (lane_k[:, :n_past] + (k0 - q0)).astype(F32),
                (TQ if n_past < n else 1, n_past)))
        if n_past < n:
            parts.append(diag_bias[grp])
        return parts[0] if len(parts) == 1 else jnp.concatenate(parts, axis=1)

    def finish(t, r):
        o = (r[0][:, :LANES] / r[0][:, LANES:]
             - lam * (r[1][:, :LANES] / r[1][:, LANES:]))
        o = o * lax.rsqrt(jnp.mean(o * o, axis=-1, keepdims=True) + RMS_EPS) * post
        o_ref[0, t.q0:t.q0 + TQ, t.grp * LANES:(t.grp + 1) * LANES] = o.astype(BF16)

    tiles = [_QueryTile(grp, qb, _key_blocks(0, (qb + 1) * TQ, TK),
                        functools.partial(bias_fn, grp, qb * TQ))
             for grp in range(N_HEADS_DIFF) for qb in range(nq)]
    _attend(tiles, qs_ref, kt_ref, vx_ref, s_ref, finish)


def _diff_attention(q, k, v, layer, slopes, lam_vecs, linit, g_diff):
    b, s, _ = q.shape
    return pl.pallas_call(
        _diff_kernel,
        grid=(b,),
        in_specs=_qkv_specs(s, DIFF_W) + [
            _const_spec((N_HEADS_DIFF, 1, LANES)),
            _layer_spec((4, HEAD_DIM), layer), _layer_spec((1, LANES), layer),
            _layer_spec((1, LANES), layer)],
        out_specs=pl.BlockSpec((1, s, DIFF_W), lambda i: (i, 0, 0)),
        out_shape=jax.ShapeDtypeStruct((b, s, DIFF_W), BF16),
        scratch_shapes=_attn_scratch(s, DIFF_W, TQ, s),
        compiler_params=_params(1),
        name="diff_attn",
    )(q, k, v, slopes, lam_vecs, linit, g_diff)


def _chk_kernel(q_ref, k_ref, v_ref, rel_ref, o_ref, qs_ref, kt_ref, vx_ref, s_ref,
                bias_ref):
    nq = q_ref.shape[1] // TQ_CHK

    @pl.when(pl.program_id(0) == 0)
    def _():
        u = lax.broadcasted_iota(jnp.int32, (REL_PAD, ROLL_W), 1)
        r = lax.broadcasted_iota(jnp.int32, (REL_PAD, ROLL_W), 0)
        u = jnp.where(u < CHK_WIN, u, u - ROLL_W)
        idx = jnp.clip(CHK_BAND - u, -(CHUNK - 1), MAX_REL_DIST) + (CHUNK - 1)
        onehot = (r == idx).astype(BF16)
        g_all = _dot3(rel_ref[...], onehot)
        ri = lax.broadcasted_iota(jnp.int32, (TQ_CHK, CHK_WIN), 0) // CHUNK
        ci = lax.broadcasted_iota(jnp.int32, (TQ_CHK, CHK_WIN), 1) // CHUNK
        band = (ci >= ri) & (ci <= ri + CHUNK_LOOKBACK)
        for h in range(N_HEADS):
            g = jnp.broadcast_to(g_all[h:h + 1], (TQ_CHK, ROLL_W))
            toep = pltpu.roll(g, 0, 1, stride=1, stride_axis=0)
            bias_ref[h] = jnp.where(band, toep[:, :CHK_WIN], NEG_INF)

    _prep_streams(q_ref, k_ref, v_ref, qs_ref, kt_ref, vx_ref, v_masked=True)

    def bias_fn(grp, q0, st, k0, n):
        w0 = k0 - (q0 - CHK_BAND)
        return bias_ref[2 * grp + st, :, w0:w0 + n]

    def finish(t, r):
        out = r[0][:, :LANES] / r[0][:, LANES:] + r[1][:, :LANES] / r[1][:, LANES:]
        o_ref[0, t.q0:t.q0 + TQ_CHK, t.grp * LANES:(t.grp + 1) * LANES] = out.astype(BF16)

    tiles = [_QueryTile(grp, qb,
                        _key_blocks(max(0, qb * TQ_CHK - CHK_BAND), (qb + 1) * TQ_CHK, CHK_WIN),
                        functools.partial(bias_fn, grp, qb * TQ_CHK), TQ_CHK)
             for grp in range(CHK_W // LANES) for qb in range(nq)]
    _attend(tiles, qs_ref, kt_ref, vx_ref, s_ref, finish)


def _chk_attention(q, k, v, layer, rel_pad):
    b, s, _ = q.shape
    return pl.pallas_call(
        _chk_kernel,
        grid=(b,),
        in_specs=_qkv_specs(s, CHK_W) + [_layer_spec((8, REL_PAD), layer)],
        out_specs=pl.BlockSpec((1, s, CHK_W), lambda i: (i, 0, 0)),
        out_shape=jax.ShapeDtypeStruct((b, s, CHK_W), BF16),
        scratch_shapes=_attn_scratch(s, CHK_W, TQ_CHK, CHK_WIN) + [
            pltpu.VMEM((N_HEADS, TQ_CHK, CHK_WIN), F32)],
        compiler_params=_params(1),
        name="chunk_attn",
    )(q, k, v, rel_pad)


def _ffn_kernel(x_ref, fox_ref, diff_ref, chk_ref, wo_ref, g_ref, wup_ref, cw_ref, cb_ref,
                wdn_ref, gfin_ref, o_ref, carry_ref, act_ref, *, final_norm):
    tm = x_ref.shape[1]
    sub_rows = tm // FFN_SUBTILES

    @pl.when(pl.program_id(1) == 0)
    def _():
        carry_ref[...] = jnp.zeros_like(carry_ref)

    sub = lax.broadcasted_iota(jnp.int32, (8, FF_CHUNK), 0)

    def shifted(d, prev, n):
        r = pltpu.roll(d, n, 0)
        head = jnp.where(sub < n, pltpu.roll(prev, n, 0), r[:8])
        return jnp.concatenate([head, r[8:]], axis=0)

    def prologue(rows):
        mix = jnp.concatenate([fox_ref[0, rows], diff_ref[0, rows], chk_ref[0, rows]], axis=1)
        x1 = x_ref[0, rows] + _dot(mix, wo_ref[...])
        return x1, _rms(x1, g_ref[...]).astype(BF16)

    def up(rows, h):
        for c in range(D_FF // FF_CHUNK):
            ys = []
            for part in range(2):
                c0 = part * D_FF + c * FF_CHUNK
                cols = slice(c0, c0 + FF_CHUNK)
                d = _dot(h, wup_ref[:, cols])
                prev = carry_ref[:, cols]
                carry_ref[:, cols] = d[d.shape[0] - 8:]
                ys.append(cb_ref[:, cols]
                          + cw_ref[2:3, cols] * d
                          + cw_ref[1:2, cols] * shifted(d, prev, 1)
                          + cw_ref[0:1, cols] * shifted(d, prev, 2))
            hg = 0.5 * ys[1]
            act_ref[rows, c * FF_CHUNK:(c + 1) * FF_CHUNK] = (
                (hg + hg * jnp.tanh(hg)) * ys[0]).astype(BF16)

    def down(rows, x1):
        acc = x1 + _dot(act_ref[rows, :], wdn_ref[...])
        o_ref[0, rows] = _rms(acc, gfin_ref[...]) if final_norm else acc

    x1, h = prologue(slice(0, tm))
    up(slice(0, tm), h)
    for i in range(FFN_SUBTILES):
        rows = slice(i * sub_rows, (i + 1) * sub_rows)
        down(rows, x1[rows])


def _out_ffn(x, fox_o, diff_o, chk_o, layer, w_out, g_ffn, w_up, conv_w, conv_b, w_down,
             g_final, final_norm):
    b, s, d = x.shape
    grid = (b, s // TM_FFN)
    row_spec = lambda n: pl.BlockSpec((1, TM_FFN, n), lambda i, j: (i, j, 0))
    single = lambda shape: _layer_spec(shape, layer, single_buffer=True)
    return pl.pallas_call(
        functools.partial(_ffn_kernel, final_norm=final_norm),
        grid=grid,
        in_specs=[row_spec(d), row_spec(FOX_W), row_spec(DIFF_W), row_spec(CHK_W),
                  single((FOX_W + DIFF_W + CHK_W, d)), _layer_spec((1, d), layer),
                  single((d, 2 * D_FF)),
                  _layer_spec((3, 2 * D_FF), layer), _layer_spec((1, 2 * D_FF), layer),
                  single((D_FF, d)), _const_spec((1, d))],
        out_specs=row_spec(d),
        out_shape=jax.ShapeDtypeStruct((b, s, d), F32),
        scratch_shapes=[pltpu.VMEM((8, 2 * D_FF), F32),
                        pltpu.VMEM((TM_FFN, D_FF), BF16)],
        compiler_params=_params(2),
        name="out_ffn",
    )(x, fox_o, diff_o, chk_o, w_out, g_ffn, w_up, conv_w, conv_b, w_down, g_final)


def kernel(x, g_mix, w_in, b_fox_f, diff_lambda, g_diff, rel_bias, w_out,
           g_ffn, w_ffn_in, conv_w, conv_b, w_ffn_out, g_final):
    depth = w_in.shape[0]
    d = x.shape[-1]
    o_ff = 3 * FOX_W
    o_rest = o_ff + N_HEADS
    slopes = jnp.asarray([2.0 ** (-8.0 * (i + 1) / N_HEADS_DIFF)
                          for i in range(N_HEADS_DIFF)], F32)
    slopes = jnp.broadcast_to(slopes[:, None, None], (N_HEADS_DIFF, 1, LANES))
    w_main = jnp.concatenate([w_in[:, :, :o_ff], w_in[:, :, o_rest:]], axis=2).astype(BF16)
    wff_t = jnp.zeros((depth, 8, d), BF16).at[:, :N_HEADS].set(
        jnp.swapaxes(w_in[:, :, o_ff:o_rest], 1, 2).astype(BF16))
    bff = jnp.zeros((depth, 8, 1), F32).at[:, :N_HEADS, 0].set(b_fox_f.astype(F32))
    linit = jnp.stack([jnp.full((1, LANES), 0.8 - 0.6 * math.exp(-0.3 * l), F32)
                       for l in range(depth)])
    rel_pad = jnp.zeros((depth, 8, REL_PAD), F32).at[:, :N_HEADS, :rel_bias.shape[2]].set(
        rel_bias.astype(F32))
    w_out_b, w_up_b, w_down_b = (w.astype(BF16) for w in (w_out, w_ffn_in, w_ffn_out))

    for l in range(depth):
        *qkv, cum = _in_proj(x, l, g_mix[:, None], w_main, wff_t, bff)
        fox_o = _fox_attention(*qkv[0:3], cum)
        diff_o = _diff_attention(*qkv[3:6], l, slopes, diff_lambda.astype(F32), linit,
                                 g_diff[:, None].astype(F32))
        chk_o = _chk_attention(*qkv[6:9], l, rel_pad)
        x = _out_ffn(x, fox_o, diff_o, chk_o, l, w_out_b, g_ffn[:, None], w_up_b, conv_w,
                     conv_b[:, None], w_down_b, g_final[None], final_norm=(l == depth - 1))
    return x
```

```python
import functools
import math

import jax
import jax.numpy as jnp
from jax import lax
from jax.experimental import pallas as pl
from jax.experimental.pallas import tpu as pltpu

F32 = jnp.float32
BF16 = jnp.bfloat16

D_MODEL = 1024
HEAD_DIM = 64
CHUNK = 64
CHUNK_LOOKBACK = 8
MAX_REL_DIST = 128
D_FF = 2816
RMS_EPS = 1e-6
NEG_INF = -1e30
N_HEADS = 4
N_HEADS_DIFF = 2
FOX_W = 256
DIFF_W = 256
CHK_W = 256
LANES = 128
REL_PAD = 256

TM_IN = 2048
TM_FFN = 1024
FF_CHUNK = 256
FFN_SUBTILES = 2
TQ = 256
TK = 512
CHK_BAND = CHUNK_LOOKBACK * CHUNK
TQ_CHK = 128
CHK_WIN = CHK_BAND + TQ_CHK
ROLL_W = 1024
VMEM_LIMIT = 56 * 1024 * 1024


def _dot(a, b):
    return jnp.dot(a, b, preferred_element_type=F32)


def _dot_nt(a, b):
    return lax.dot_general(a, b, (((1,), (1,)), ((), ())), preferred_element_type=F32)


def _split3(x):
    hi = x.astype(BF16)
    r1 = x - hi.astype(F32)
    mid = r1.astype(BF16)
    lo = (r1 - mid.astype(F32)).astype(BF16)
    return hi, mid, lo


def _dot3(x, t):
    hi, mid, lo = _split3(x)
    return _dot(hi, t) + _dot(mid, t) + _dot(lo, t)


def _rms(x, g):
    return x * lax.rsqrt(jnp.mean(x * x, axis=-1, keepdims=True) + RMS_EPS) * g


def _const_spec(shape):
    nd = len(shape)
    return pl.BlockSpec(shape, lambda *_: (0,) * nd)


def _layer_spec(shape, layer, single_buffer=False):
    nd = len(shape)
    return pl.BlockSpec((None,) + tuple(shape), lambda *_: (layer,) + (0,) * nd,
                        pipeline_mode=pl.Buffered(1) if single_buffer else None)


def _params(n_grid):
    return pltpu.CompilerParams(
        dimension_semantics=("arbitrary",) * n_grid, vmem_limit_bytes=VMEM_LIMIT)


def _in_proj_kernel(x_ref, g_ref, w_ref, wff_ref, bff_ref,
                    *out_and_scratch):
    *qkv_refs, cum_ref, carry_ref = out_and_scratch
    tm = x_ref.shape[1]
    h = _rms(x_ref[0], g_ref[...]).astype(BF16)

    ff = _dot_nt(wff_ref[...], h) + bff_ref[...]
    log_f = jax.nn.log_sigmoid(ff)

    @pl.when(pl.program_id(1) == 0)
    def _():
        carry_ref[...] = jnp.zeros_like(carry_ref)

    n_grp = tm // LANES
    row = lax.broadcasted_iota(jnp.int32, (LANES, 2 * LANES), 0)
    col = lax.broadcasted_iota(jnp.int32, (LANES, 2 * LANES), 1)
    tri_ones = ((row <= col) | (col >= LANES)).astype(BF16)
    grouped = jnp.concatenate(
        [log_f[:, g * LANES:(g + 1) * LANES] for g in range(n_grp)], axis=0)
    r = _dot(jnp.concatenate(_split3(grouped), axis=0), tri_ones)
    r = r[:8 * n_grp] + r[8 * n_grp:16 * n_grp] + r[16 * n_grp:]
    run = carry_ref[...]
    pieces = []
    for g in range(n_grp):
        pieces.append(r[8 * g:8 * g + 8, :LANES] + run)
        run = run + r[8 * g:8 * g + 8, LANES:]
    cum_ref[0] = jnp.concatenate(pieces, axis=1)
    carry_ref[...] = run

    c0 = 0
    for o_ref in qkv_refs:
        n = o_ref.shape[2]
        o_ref[0] = _dot(h, w_ref[:, c0:c0 + n]).astype(BF16)
        c0 += n


def _in_proj(x, layer, g, w_main, wff_t, bff):
    b, s, d = x.shape
    n_main = w_main.shape[2]
    widths = [FOX_W] * 3 + [DIFF_W] * 3 + [CHK_W] * 3
    assert n_main == sum(widths)
    grid = (b, s // TM_IN)
    row_spec = lambda n: pl.BlockSpec((1, TM_IN, n), lambda i, j: (i, j, 0))
    return pl.pallas_call(
        _in_proj_kernel,
        grid=grid,
        in_specs=[row_spec(d), _layer_spec((1, d), layer), _layer_spec((d, n_main), layer),
                  _layer_spec((8, d), layer), _layer_spec((8, 1), layer)],
        out_specs=[row_spec(n) for n in widths] + [
            pl.BlockSpec((1, 8, TM_IN), lambda i, j: (i, 0, j))],
        out_shape=[jax.ShapeDtypeStruct((b, s, n), BF16) for n in widths] + [
            jax.ShapeDtypeStruct((b, 8, s), F32)],
        scratch_shapes=[pltpu.VMEM((8, LANES), F32)],
        compiler_params=_params(2),
        name="in_proj",
    )(x, g, w_main, wff_t, bff)


def _half_masks():
    lane = lax.broadcasted_iota(jnp.int32, (1, LANES), 1)
    return lane < HEAD_DIM, lane >= HEAD_DIM


def _prep_streams(q_ref, k_ref, v_ref, qs_ref, kt_ref, vx_ref, v_masked):
    for grp in range(q_ref.shape[2] // LANES):
        lanes = slice(grp * LANES, (grp + 1) * LANES)
        kt_ref[grp] = k_ref[0, :, lanes].T
        q = q_ref[0, :, lanes] * jnp.asarray(HEAD_DIM ** -0.5, BF16)
        v = v_ref[0, :, lanes]
        for st, m in enumerate(_half_masks()):
            qs_ref[2 * grp + st] = jnp.where(m, q, jnp.zeros_like(q))
            vx_ref[2 * grp + st, :, :LANES] = (
                jnp.where(m, v, jnp.zeros_like(v)) if v_masked else v)
            vx_ref[2 * grp + st, :, LANES:] = jnp.ones_like(v)


def _fold_max(s):
    m = s[:, :LANES]
    for j in range(1, s.shape[1] // LANES):
        m = jnp.maximum(m, s[:, j * LANES:(j + 1) * LANES])
    return m


def _key_blocks(k_lo, k_hi, width):
    return [(k0, min(width, k_hi - k0)) for k0 in range(k_lo, k_hi, width)]


class _QueryTile:
    def __init__(self, grp, qb, blocks, bias_fn, tq=TQ):
        self.grp = grp
        self.tq = tq
        self.q0 = qb * tq
        self.blocks = blocks
        self.offs = [sum(n for _, n in blocks[:j]) for j in range(len(blocks))]
        self.bias_fn = bias_fn
        self.m = [None, None]
        self.mb = [None, None]
        self.acc = [None, None]


def _attend(tiles, qs_ref, kt_ref, vx_ref, s_ref, finish):
    for i, t in enumerate(tiles):
        t.slot = 2 * (i % 2)

    def pass_a(t, j):
        k0, n = t.blocks[j]
        kt = kt_ref[t.grp, :, k0:k0 + n]
        for st in range(2):
            s = (_dot(qs_ref[2 * t.grp + st, t.q0:t.q0 + t.tq, :], kt)
                 + t.bias_fn(st, k0, n))
            s_ref[t.slot + st, :, t.offs[j]:t.offs[j] + n] = s
            fm = _fold_max(s)
            t.m[st] = fm if t.m[st] is None else jnp.maximum(t.m[st], fm)

    def end_a(t):
        for st in range(2):
            t.mb[st] = jnp.broadcast_to(
                jnp.max(t.m[st], axis=-1, keepdims=True), (t.tq, LANES))

    def pass_b(t, j):
        k0, n = t.blocks[j]
        for st in range(2):
            s = s_ref[t.slot + st, :, t.offs[j]:t.offs[j] + n]
            e = jnp.exp(s - jnp.tile(t.mb[st], (1, n // LANES)))
            r = _dot(e.astype(BF16), vx_ref[2 * t.grp + st, k0:k0 + n, :])
            t.acc[st] = r if t.acc[st] is None else t.acc[st] + r

    for j in range(len(tiles[0].blocks)):
        pass_a(tiles[0], j)
    end_a(tiles[0])
    for i, t in enumerate(tiles):
        nxt = tiles[i + 1] if i + 1 < len(tiles) else None
        n_a = len(nxt.blocks) if nxt else 0
        for j in range(max(n_a, len(t.blocks))):
            if j < n_a:
                pass_a(nxt, j)
            if j < len(t.blocks):
                pass_b(t, j)
        if nxt:
            end_a(nxt)
        finish(t, t.acc)


def _attn_scratch(s, width, tq, s_cols):
    n_stream = 2 * (width // LANES)
    return [pltpu.VMEM((n_stream, s, LANES), BF16),
            pltpu.VMEM((width // LANES, LANES, s), BF16),
            pltpu.VMEM((n_stream, s, 2 * LANES), BF16),
            pltpu.VMEM((4, tq, s_cols), F32)]


def _qkv_specs(s, width):
    return [pl.BlockSpec((1, s, width), lambda i: (i, 0, 0))] * 3


def _fox_kernel(q_ref, k_ref, v_ref, cum_ref, o_ref, qs_ref, kt_ref, vx_ref, s_ref):
    nq = q_ref.shape[1] // TQ
    _prep_streams(q_ref, k_ref, v_ref, qs_ref, kt_ref, vx_ref, v_masked=True)
    ri = lax.broadcasted_iota(jnp.int32, (TQ, TQ), 0)
    ci = lax.broadcasted_iota(jnp.int32, (TQ, TQ), 1)
    causal_bias = jnp.where(ci <= ri, 0.0, NEG_INF).astype(F32)
    neg_cum = [-cum_ref[0, h:h + 1, :] for h in range(N_HEADS)]

    def bias_fn(grp, q0, st, k0, n):
        b = neg_cum[2 * grp + st][:, k0:k0 + n]
        if k0 + n <= q0:
            return b
        pad = [jnp.zeros((TQ, q0 - k0), F32)] if k0 < q0 else []
        return b + jnp.concatenate(pad + [causal_bias], axis=1)

    def finish(t, r):
        out = r[0][:, :LANES] / r[0][:, LANES:] + r[1][:, :LANES] / r[1][:, LANES:]
        o_ref[0, t.q0:t.q0 + TQ, t.grp * LANES:(t.grp + 1) * LANES] = out.astype(BF16)

    tiles = [_QueryTile(grp, qb, _key_blocks(0, (qb + 1) * TQ, TK),
                        functools.partial(bias_fn, grp, qb * TQ))
             for grp in range(FOX_W // LANES) for qb in range(nq)]
    _attend(tiles, qs_ref, kt_ref, vx_ref, s_ref, finish)


def _fox_attention(q, k, v, cum):
    b, s, _ = q.shape
    return pl.pallas_call(
        _fox_kernel,
        grid=(b,),
        in_specs=_qkv_specs(s, FOX_W) + [pl.BlockSpec((1, 8, s), lambda i: (i, 0, 0))],
        out_specs=pl.BlockSpec((1, s, FOX_W), lambda i: (i, 0, 0)),
        out_shape=jax.ShapeDtypeStruct((b, s, FOX_W), BF16),
        scratch_shapes=_attn_scratch(s, FOX_W, TQ, s),
        compiler_params=_params(1),
        name="fox_attn",
    )(q, k, v, cum)


def _diff_kernel(q_ref, k_ref, v_ref, slope_ref, lam_ref, linit_ref, g_ref, o_ref,
                 qs_ref, kt_ref, vx_ref, s_ref):
    nq = q_ref.shape[1] // TQ
    _prep_streams(q_ref, k_ref, v_ref, qs_ref, kt_ref, vx_ref, v_masked=False)
    ri = lax.broadcasted_iota(jnp.int32, (TQ, TQ), 0)
    ci = lax.broadcasted_iota(jnp.int32, (TQ, TQ), 1)
    allowed = (ci // CHUNK) <= (ri // CHUNK)
    rel = jnp.where(ci <= ri, ci, 2 * ri - ci).astype(F32)
    lane_k = lax.broadcasted_iota(jnp.int32, (1, TK), 1)
    slope_t, diag_bias = [], []
    for h in range(N_HEADS_DIFF):
        slope_t.append(jnp.tile(slope_ref[h], (1, TK // LANES)))
        diag_bias.append(jnp.where(allowed, slope_t[h][:, :TQ] * rel, NEG_INF))
    lv = lam_ref[...]
    lam = (jnp.exp(jnp.sum(lv[0:1] * lv[1:2], axis=1, keepdims=True))
           - jnp.exp(jnp.sum(lv[2:3] * lv[3:4], axis=1, keepdims=True))
           + linit_ref[...])
    post = g_ref[...] * (1.0 - linit_ref[...])

    def bias_fn(grp, q0, st, k0, n):
        n_past = min(n, q0 - k0)
        parts = []
        if n_past > 0:
            parts.append(jnp.broadcast_to(
                slope_t[grp][:, :n_past] * (lane_k[:, :n_past] + (k0 - q0)).astype(F32),
                (TQ if n_past < n else 1, n_past)))
        if n_past < n:
            parts.append(diag_bias[grp])
        return parts[0] if len(parts) == 1 else jnp.concatenate(parts, axis=1)

    def finish(t, r):
        o = (r[0][:, :LANES] / r[0][:, LANES:]
             - lam * (r[1][:, :LANES] / r[1][:, LANES:]))
        o = o * lax.rsqrt(jnp.mean(o * o, axis=-1, keepdims=True) + RMS_EPS) * post
        o_ref[0, t.q0:t.q0 + TQ, t.grp * LANES:(t.grp + 1) * LANES] = o.astype(BF16)

    tiles = [_QueryTile(grp, qb, _key_blocks(0, (qb + 1) * TQ, TK),
                        functools.partial(bias_fn, grp, qb * TQ))
             for grp in range(N_HEADS_DIFF) for qb in range(nq)]
    _attend(tiles, qs_ref, kt_ref, vx_ref, s_ref, finish)


def _diff_attention(q, k, v, layer, slopes, lam_vecs, linit, g_diff):
    b, s, _ = q.shape
    return pl.pallas_call(
        _diff_kernel,
        grid=(b,),
        in_specs=_qkv_specs(s, DIFF_W) + [
            _const_spec((N_HEADS_DIFF, 1, LANES)),
            _layer_spec((4, HEAD_DIM), layer), _layer_spec((1, LANES), layer),
            _layer_spec((1, LANES), layer)],
        out_specs=pl.BlockSpec((1, s, DIFF_W), lambda i: (i, 0, 0)),
        out_shape=jax.ShapeDtypeStruct((b, s, DIFF_W), BF16),
        scratch_shapes=_attn_scratch(s, DIFF_W, TQ, s),
        compiler_params=_params(1),
        name="diff_attn",
    )(q, k, v, slopes, lam_vecs, linit, g_diff)


def _chk_kernel(q_ref, k_ref, v_ref, rel_ref, o_ref, qs_ref, kt_ref, vx_ref, s_ref,
                bias_ref):
    nq = q_ref.shape[1] // TQ_CHK

    @pl.when(pl.program_id(0) == 0)
    def _():
        u = lax.broadcasted_iota(jnp.int32, (REL_PAD, ROLL_W), 1)
        r = lax.broadcasted_iota(jnp.int32, (REL_PAD, ROLL_W), 0)
        u = jnp.where(u < CHK_WIN, u, u - ROLL_W)
        idx = jnp.clip(CHK_BAND - u, -(CHUNK - 1), MAX_REL_DIST) + (CHUNK - 1)
        onehot = (r == idx).astype(BF16)
        g_all = _dot3(rel_ref[...], onehot)
        ri = lax.broadcasted_iota(jnp.int32, (TQ_CHK, CHK_WIN), 0) // CHUNK
        ci = lax.broadcasted_iota(jnp.int32, (TQ_CHK, CHK_WIN), 1) // CHUNK
        band = (ci >= ri) & (ci <= ri + CHUNK_LOOKBACK)
        for h in range(N_HEADS):
            g = jnp.broadcast_to(g_all[h:h + 1], (TQ_CHK, ROLL_W))
            toep = pltpu.roll(g, 0, 1, stride=1, stride_axis=0)
            bias_ref[h] = jnp.where(band, toep[:, :CHK_WIN], NEG_INF)

    _prep_streams(q_ref, k_ref, v_ref, qs_ref, kt_ref, vx_ref, v_masked=True)

    def bias_fn(grp, q0, st, k0, n):
        w0 = k0 - (q0 - CHK_BAND)
        return bias_ref[2 * grp + st, :, w0:w0 + n]

    def finish(t, r):
        out = r[0][:, :LANES] / r[0][:, LANES:] + r[1][:, :LANES] / r[1][:, LANES:]
        o_ref[0, t.q0:t.q0 + TQ_CHK, t.grp * LANES:(t.grp + 1) * LANES] = out.astype(BF16)

    tiles = [_QueryTile(grp, qb,
                        _key_blocks(max(0, qb * TQ_CHK - CHK_BAND), (qb + 1) * TQ_CHK, CHK_WIN),
                        functools.partial(bias_fn, grp, qb * TQ_CHK), TQ_CHK)
             for grp in range(CHK_W // LANES) for qb in range(nq)]
    _attend(tiles, qs_ref, kt_ref, vx_ref, s_ref, finish)


def _chk_attention(q, k, v, layer, rel_pad):
    b, s, _ = q.shape
    return pl.pallas_call(
        _chk_kernel,
        grid=(b,),
        in_specs=_qkv_specs(s, CHK_W) + [_layer_spec((8, REL_PAD), layer)],
        out_specs=pl.BlockSpec((1, s, CHK_W), lambda i: (i, 0, 0)),
        out_shape=jax.ShapeDtypeStruct((b, s, CHK_W), BF16),
        scratch_shapes=_attn_scratch(s, CHK_W, TQ_CHK, CHK_WIN) + [
            pltpu.VMEM((N_HEADS, TQ_CHK, CHK_WIN), F32)],
        compiler_params=_params(1),
        name="chunk_attn",
    )(q, k, v, rel_pad)


def _ffn_kernel(x_ref, fox_ref, diff_ref, chk_ref, wo_ref, g_ref, wup_ref, cw_ref, cb_ref,
                wdn_ref, gfin_ref, o_ref, carry_ref, act_ref, *, final_norm):
    tm = x_ref.shape[1]
    sub_rows = tm // FFN_SUBTILES

    @pl.when(pl.program_id(1) == 0)
    def _():
        carry_ref[...] = jnp.zeros_like(carry_ref)

    sub = lax.broadcasted_iota(jnp.int32, (8, FF_CHUNK), 0)

    def shifted(d, prev, n):
        r = pltpu.roll(d, n, 0)
        head = jnp.where(sub < n, pltpu.roll(prev, n, 0), r[:8])
        return jnp.concatenate([head, r[8:]], axis=0)

    def prologue(rows):
        mix = jnp.concatenate([fox_ref[0, rows], diff_ref[0, rows], chk_ref[0, rows]], axis=1)
        x1 = x_ref[0, rows] + _dot(mix, wo_ref[...])
        return x1, _rms(x1, g_ref[...]).astype(BF16)

    def up(rows, h):
        for c in range(D_FF // FF_CHUNK):
            ys = []
            for part in range(2):
                c0 = part * D_FF + c * FF_CHUNK
                cols = slice(c0, c0 + FF_CHUNK)
                d = _dot(h, wup_ref[:, cols])
                prev = carry_ref[:, cols]
                carry_ref[:, cols] = d[d.shape[0] - 8:]
                ys.append(cb_ref[:, cols]
                          + cw_ref[2:3, cols] * d
                          + cw_ref[1:2, cols] * shifted(d, prev, 1)
                          + cw_ref[0:1, cols] * shifted(d, prev, 2))
            hg = 0.5 * ys[1]
            act_ref[rows, c * FF_CHUNK:(c + 1) * FF_CHUNK] = (
                (hg + hg * jnp.tanh(hg)) * ys[0]).astype(BF16)

    def down(rows, x1):
        acc = x1 + _dot(act_ref[rows, :], wdn_ref[...])
        o_ref[0, rows] = _rms(acc, gfin_ref[...]) if final_norm else acc

    x1, h = prologue(slice(0, tm))
    up(slice(0, tm), h)
    for i in range(FFN_SUBTILES):
        rows = slice(i * sub_rows, (i + 1) * sub_rows)
        down(rows, x1[rows])


def _out_ffn(x, fox_o, diff_o, chk_o, layer, w_out, g_ffn, w_up, conv_w, conv_b, w_down,
             g_final, final_norm):
    b, s, d = x.shape
    grid = (b, s // TM_FFN)
    row_spec = lambda n: pl.BlockSpec((1, TM_FFN, n), lambda i, j: (i, j, 0))
    single = lambda shape: _layer_spec(shape, layer, single_buffer=True)
    return pl.pallas_call(
        functools.partial(_ffn_kernel, final_norm=final_norm),
        grid=grid,
        in_specs=[row_spec(d), row_spec(FOX_W), row_spec(DIFF_W), row_spec(CHK_W),
                  single((FOX_W + DIFF_W + CHK_W, d)), _layer_spec((1, d), layer),
                  single((d, 2 * D_FF)),
                  _layer_spec((3, 2 * D_FF), layer), _layer_spec((1, 2 * D_FF), layer),
                  single((D_FF, d)), _const_spec((1, d))],
        out_specs=row_spec(d),
        out_shape=jax.ShapeDtypeStruct((b, s, d), F32),
        scratch_shapes=[pltpu.VMEM((8, 2 * D_FF), F32),
                        pltpu.VMEM((TM_FFN, D_FF), BF16)],
        compiler_params=_params(2),
        name="out_ffn",
    )(x, fox_o, diff_o, chk_o, w_out, g_ffn, w_up, conv_w, conv_b, w_down, g_final)


def kernel(x, g_mix, w_in, b_fox_f, diff_lambda, g_diff, rel_bias, w_out,
           g_ffn, w_ffn_in, conv_w, conv_b, w_ffn_out, g_final):
    depth = w_in.shape[0]
    d = x.shape[-1]
    o_ff = 3 * FOX_W
    o_rest = o_ff + N_HEADS
    slopes = jnp.asarray([2.0 ** (-8.0 * (i + 1) / N_HEADS_DIFF)
                          for i in range(N_HEADS_DIFF)], F32)
    slopes = jnp.broadcast_to(slopes[:, None, None], (N_HEADS_DIFF, 1, LANES))
    w_main = jnp.concatenate([w_in[:, :, :o_ff], w_in[:, :, o_rest:]], axis=2).astype(BF16)
    wff_t = jnp.zeros((depth, 8, d), BF16).at[:, :N_HEADS].set(
        jnp.swapaxes(w_in[:, :, o_ff:o_rest], 1, 2).astype(BF16))
    bff = jnp.zeros((depth, 8, 1), F32).at[:, :N_HEADS, 0].set(b_fox_f.astype(F32))
    linit = jnp.stack([jnp.full((1, LANES), 0.8 - 0.6 * math.exp(-0.3 * l), F32)
                       for l in range(depth)])
    rel_pad = jnp.zeros((depth, 8, REL_PAD), F32).at[:, :N_HEADS, :rel_bias.shape[2]].set(
        rel_bias.astype(F32))
    w_out_b, w_up_b, w_down_b = (w.astype(BF16) for w in (w_out, w_ffn_in, w_ffn_out))

    for l in range(depth):
        *qkv, cum = _in_proj(x, l, g_mix[:, None], w_main, wff_t, bff)
        fox_o = _fox_attention(*qkv[0:3], cum)
        diff_o = _diff_attention(*qkv[3:6], l, slopes, diff_lambda.astype(F32), linit,
                                 g_diff[:, None].astype(F32))
        chk_o = _chk_attention(*qkv[6:9], l, rel_pad)
        x = _out_ffn(x, fox_o, diff_o, chk_o, l, w_out_b, g_ffn[:, None], w_up_b, conv_w,
                     conv_b[:, None], w_down_b, g_final[None], final_norm=(l == depth - 1))
    return x
```

```python
import functools
import math

import jax
import jax.numpy as jnp
from jax import lax
from jax.experimental import pallas as pl
from jax.experimental.pallas import tpu as pltpu

F32 = jnp.float32
BF16 = jnp.bfloat16

D_MODEL = 1024
HEAD_DIM = 64
CHUNK = 64
CHUNK_LOOKBACK = 8
MAX_REL_DIST = 128
D_FF = 2816
RMS_EPS = 1e-6
NEG_INF = -1e30
N_HEADS = 4
N_HEADS_DIFF = 2
FOX_W = 256
DIFF_W = 256
CHK_W = 256
LANES = 128
REL_PAD = 256

TM_IN = 2048
TM_FFN = 1024
FF_CHUNK = 256
FFN_SUBTILES = 2
TQ = 256
TK = 512
CHK_BAND = CHUNK_LOOKBACK * CHUNK
TQ_CHK = 128
CHK_WIN = CHK_BAND + TQ_CHK
ROLL_W = 1024
VMEM_LIMIT = 56 * 1024 * 1024


def _dot(a, b):
    return jnp.dot(a, b, preferred_element_type=F32)


def _dot_nt(a, b):
    return lax.dot_general(a, b, (((1,), (1,)), ((), ())), preferred_element_type=F32)


def _split3(x):
    hi = x.astype(BF16)
    r1 = x - hi.astype(F32)
    mid = r1.astype(BF16)
    lo = (r1 - mid.astype(F32)).astype(BF16)
    return hi, mid, lo


def _dot3(x, t):
    hi, mid, lo = _split3(x)
    return _dot(hi, t) + _dot(mid, t) + _dot(lo, t)


def _rms(x, g):
    return x * lax.rsqrt(jnp.mean(x * x, axis=-1, keepdims=True) + RMS_EPS) * g


def _const_spec(shape):
    nd = len(shape)
    return pl.BlockSpec(shape, lambda *_: (0,) * nd)


def _layer_spec(shape, layer, single_buffer=False):
    nd = len(shape)
    return pl.BlockSpec((None,) + tuple(shape), lambda *_: (layer,) + (0,) * nd,
                        pipeline_mode=pl.Buffered(1) if single_buffer else None)


def _params(n_grid):
    return pltpu.CompilerParams(
        dimension_semantics=("arbitrary",) * n_grid, vmem_limit_bytes=VMEM_LIMIT)


def _in_proj_kernel(x_ref, g_ref, w_ref, wff_ref, bff_ref,
                    *out_and_scratch):
    *qkv_refs, cum_ref, carry_ref = out_and_scratch
    tm = x_ref.shape[1]
    h = _rms(x_ref[0], g_ref[...]).astype(BF16)

    ff = _dot_nt(wff_ref[...], h) + bff_ref[...]
    log_f = jax.nn.log_sigmoid(ff)

    @pl.when(pl.program_id(1) == 0)
    def _():
        carry_ref[...] = jnp.zeros_like(carry_ref)

    n_grp = tm // LANES
    row = lax.broadcasted_iota(jnp.int32, (LANES, 2 * LANES), 0)
    col = lax.broadcasted_iota(jnp.int32, (LANES, 2 * LANES), 1)
    tri_ones = ((row <= col) | (col >= LANES)).astype(BF16)
    grouped = jnp.concatenate(
        [log_f[:, g * LANES:(g + 1) * LANES] for g in range(n_grp)], axis=0)
    r = _dot(jnp.concatenate(_split3(grouped), axis=0), tri_ones)
    r = r[:8 * n_grp] + r[8 * n_grp:16 * n_grp] + r[16 * n_grp:]
    run = carry_ref[...]
    pieces = []
    for g in range(n_grp):
        pieces.append(r[8 * g:8 * g + 8, :LANES] + run)
        run = run + r[8 * g:8 * g + 8, LANES:]
    cum_ref[0] = jnp.concatenate(pieces, axis=1)
    carry_ref[...] = run

    c0 = 0
    for o_ref in qkv_refs:
        n = o_ref.shape[2]
        o_ref[0] = _dot(h, w_ref[:, c0:c0 + n]).astype(BF16)
        c0 += n


def _in_proj(x, layer, g, w_main, wff_t, bff):
    b, s, d = x.shape
    n_main = w_main.shape[2]
    widths = [FOX_W] * 3 + [DIFF_W] * 3 + [CHK_W] * 3
    assert n_main == sum(widths)
    grid = (b, s // TM_IN)
    row_spec = lambda n: pl.BlockSpec((1, TM_IN, n), lambda i, j: (i, j, 0))
    return pl.pallas_call(
        _in_proj_kernel,
        grid=grid,
        in_specs=[row_spec(d), _layer_spec((1, d), layer), _layer_spec((d, n_main), layer),
                  _layer_spec((8, d), layer), _layer_spec((8, 1), layer)],
        out_specs=[row_spec(n) for n in widths] + [
            pl.BlockSpec((1, 8, TM_IN), lambda i, j: (i, 0, j))],
        out_shape=[jax.ShapeDtypeStruct((b, s, n), BF16) for n in widths] + [
            jax.ShapeDtypeStruct((b, 8, s), F32)],
        scratch_shapes=[pltpu.VMEM((8, LANES), F32)],
        compiler_params=_params(2),
        name="in_proj",
    )(x, g, w_main, wff_t, bff)


def _half_masks():
    lane = lax.broadcasted_iota(jnp.int32, (1, LANES), 1)
    return lane < HEAD_DIM, lane >= HEAD_DIM


def _prep_streams(q_ref, k_ref, v_ref, qs_ref, kt_ref, vx_ref, v_masked):
    for grp in range(q_ref.shape[2] // LANES):
        lanes = slice(grp * LANES, (grp + 1) * LANES)
        kt_ref[grp] = k_ref[0, :, lanes].T
        q = q_ref[0, :, lanes] * jnp.asarray(HEAD_DIM ** -0.5, BF16)
        v = v_ref[0, :, lanes]
        for st, m in enumerate(_half_masks()):
            qs_ref[2 * grp + st] = jnp.where(m, q, jnp.zeros_like(q))
            vx_ref[2 * grp + st, :, :LANES] = (
                jnp.where(m, v, jnp.zeros_like(v)) if v_masked else v)
            vx_ref[2 * grp + st, :, LANES:] = jnp.ones_like(v)


def _fold_max(s):
    m = s[:, :LANES]
    for j in range(1, s.shape[1] // LANES):
        m = jnp.maximum(m, s[:, j * LANES:(j + 1) * LANES])
    return m


def _key_blocks(k_lo, k_hi, width):
    return [(k0, min(width, k_hi - k0)) for k0 in range(k_lo, k_hi, width)]


class _QueryTile:
    def __init__(self, grp, qb, blocks, bias_fn, tq=TQ):
        self.grp = grp
        self.tq = tq
        self.q0 = qb * tq
        self.blocks = blocks
        self.offs = [sum(n for _, n in blocks[:j]) for j in range(len(blocks))]
        self.bias_fn = bias_fn
        self.m = [None, None]
        self.mb = [None, None]
        self.acc = [None, None]


def _attend(tiles, qs_ref, kt_ref, vx_ref, s_ref, finish):
    for i, t in enumerate(tiles):
        t.slot = 2 * (i % 2)

    def pass_a(t, j):
        k0, n = t.blocks[j]
        kt = kt_ref[t.grp, :, k0:k0 + n]
        for st in range(2):
            s = (_dot(qs_ref[2 * t.grp + st, t.q0:t.q0 + t.tq, :], kt)
                 + t.bias_fn(st, k0, n))
            s_ref[t.slot + st, :, t.offs[j]:t.offs[j] + n] = s
            fm = _fold_max(s)
            t.m[st] = fm if t.m[st] is None else jnp.maximum(t.m[st], fm)

    def end_a(t):
        for st in range(2):
            t.mb[st] = jnp.broadcast_to(
                jnp.max(t.m[st], axis=-1, keepdims=True), (t.tq, LANES))

    def pass_b(t, j):
        k0, n = t.blocks[j]
        for st in range(2):
            s = s_ref[t.slot + st, :, t.offs[j]:t.offs[j] + n]
            e = jnp.exp(s - jnp.tile(t.mb[st], (1, n // LANES)))
            r = _dot(e.astype(BF16), vx_ref[2 * t.grp + st, k0:k0 + n, :])
            t.acc[st] = r if t.acc[st] is None else t.acc[st] + r

    for j in range(len(tiles[0].blocks)):
        pass_a(tiles[0], j)
    end_a(tiles[0])
    for i, t in enumerate(tiles):
        nxt = tiles[i + 1] if i + 1 < len(tiles) else None
        n_a = len(nxt.blocks) if nxt else 0
        for j in range(max(n_a, len(t.blocks))):
            if j < n_a:
                pass_a(nxt, j)
            if j < len(t.blocks):
                pass_b(t, j)
        if nxt:
            end_a(nxt)
        finish(t, t.acc)


def _attn_scratch(s, width, tq, s_cols):
    n_stream = 2 * (width // LANES)
    return [pltpu.VMEM((n_stream, s, LANES), BF16),
            pltpu.VMEM((width // LANES, LANES, s), BF16),
            pltpu.VMEM((n_stream, s, 2 * LANES), BF16),
            pltpu.VMEM((4, tq, s_cols), F32)]


def _qkv_specs(s, width):
    return [pl.BlockSpec((1, s, width), lambda i: (i, 0, 0))] * 3


def _fox_kernel(q_ref, k_ref, v_ref, cum_ref, o_ref, qs_ref, kt_ref, vx_ref, s_ref):
    nq = q_ref.shape[1] // TQ
    _prep_streams(q_ref, k_ref, v_ref, qs_ref, kt_ref, vx_ref, v_masked=True)
    ri = lax.broadcasted_iota(jnp.int32, (TQ, TQ), 0)
    ci = lax.broadcasted_iota(jnp.int32, (TQ, TQ), 1)
    causal_bias = jnp.where(ci <= ri, 0.0, NEG_INF).astype(F32)
    neg_cum = [-cum_ref[0, h:h + 1, :] for h in range(N_HEADS)]

    def bias_fn(grp, q0, st, k0, n):
        b = neg_cum[2 * grp + st][:, k0:k0 + n]
        if k0 + n <= q0:
            return b
        pad = [jnp.zeros((TQ, q0 - k0), F32)] if k0 < q0 else []
        return b + jnp.concatenate(pad + [causal_bias], axis=1)

    def finish(t, r):
        out = r[0][:, :LANES] / r[0][:, LANES:] + r[1][:, :LANES] / r[1][:, LANES:]
        o_ref[0, t.q0:t.q0 + TQ, t.grp * LANES:(t.grp + 1) * LANES] = out.astype(BF16)

    tiles = [_QueryTile(grp, qb, _key_blocks(0, (qb + 1) * TQ, TK),
                        functools.partial(bias_fn, grp, qb * TQ))
             for grp in range(FOX_W // LANES) for qb in range(nq)]
    _attend(tiles, qs_ref, kt_ref, vx_ref, s_ref, finish)


def _fox_attention(q, k, v, cum):
    b, s, _ = q.shape
    return pl.pallas_call(
        _fox_kernel,
        grid=(b,),
        in_specs=_qkv_specs(s, FOX_W) + [pl.BlockSpec((1, 8, s), lambda i: (i, 0, 0))],
        out_specs=pl.BlockSpec((1, s, FOX_W), lambda i: (i, 0, 0)),
        out_shape=jax.ShapeDtypeStruct((b, s, FOX_W), BF16),
        scratch_shapes=_attn_scratch(s, FOX_W, TQ, s),
        compiler_params=_params(1),
        name="fox_attn",
    )(q, k, v, cum)


def _diff_kernel(q_ref, k_ref, v_ref, slope_ref, lam_ref, linit_ref, g_ref, o_ref,
                 qs_ref, kt_ref, vx_ref, s_ref):
    nq = q_ref.shape[1] // TQ
    _prep_streams(q_ref, k_ref, v_ref, qs_ref, kt_ref, vx_ref, v_masked=False)
    ri = lax.broadcasted_iota(jnp.int32, (TQ, TQ), 0)
    ci = lax.broadcasted_iota(jnp.int32, (TQ, TQ), 1)
    allowed = (ci // CHUNK) <= (ri // CHUNK)
    rel = jnp.where(ci <= ri, ci, 2 * ri - ci).astype(F32)
    lane_k = lax.broadcasted_iota(jnp.int32, (1, TK), 1)
    slope_t, diag_bias = [], []
    for h in range(N_HEADS_DIFF):
        slope_t.append(jnp.tile(slope_ref[h], (1, TK // LANES)))
        diag_bias.append(jnp.where(allowed, slope_t[h][:, :TQ] * rel, NEG_INF))
    lv = lam_ref[...]
    lam = (jnp.exp(jnp.sum(lv[0:1] * lv[1:2], axis=1, keepdims=True))
           - jnp.exp(jnp.sum(lv[2:3] * lv[3:4], axis=1, keepdims=True))
           + linit_ref[...])
    post = g_ref[...] * (1.0 - linit_ref[...])

    def bias_fn(grp, q0, st, k0, n):
        n_past = min(n, q0 - k0)
        parts = []
        if n_past > 0:
            parts.append(jnp.broadcast_to(
                slope_t[grp][:, :n_past] * (lane_k[:, :n_past] + (k0 - q0)).astype(F32),
                (TQ if n_past < n else 1, n_past)))
        if n_past < n:
            parts.append(diag_bias[grp])
        return parts[0] if len(parts) == 1 else jnp.concatenate(parts, axis=1)

    def finish(t, r):
        o = (r[0][:, :LANES] / r[0][:, LANES:]
             - lam * (r[1][:, :LANES] / r[1][:, LANES:]))
        o = o * lax.rsqrt(jnp.mean(o * o, axis=-1, keepdims=True) + RMS_EPS) * post
        o_ref[0, t.q0:t.q0 + TQ, t.grp * LANES:(t.grp + 1) * LANES] = o.astype(BF16)

    tiles = [_QueryTile(grp, qb, _key_blocks(0, (qb + 1) * TQ, TK),
                        functools.partial(bias_fn, grp, qb * TQ))
             for grp in range(N_HEADS_DIFF) for qb in range(nq)]
    _attend(tiles, qs_ref, kt_ref, vx_ref, s_ref, finish)


def _diff_attention(q, k, v, layer, slopes, lam_vecs, linit, g_diff):
    b, s, _ = q.shape
    return pl.pallas_call(
        _diff_kernel,
        grid=(b,),
        in_specs=_qkv_specs(s, DIFF_W) + [
            _const_spec((N_HEADS_DIFF, 1, LANES)),
            _layer_spec((4, HEAD_DIM), layer), _layer_spec((1, LANES), layer),
            _layer_spec((1, LANES), layer)],
        out_specs=pl.BlockSpec((1, s, DIFF_W), lambda i: (i, 0, 0)),
        out_shape=jax.ShapeDtypeStruct((b, s, DIFF_W), BF16),
        scratch_shapes=_attn_scratch(s, DIFF_W, TQ, s),
        compiler_params=_params(1),
        name="diff_attn",
    )(q, k, v, slopes, lam_vecs, linit, g_diff)


def _chk_kernel(q_ref, k_ref, v_ref, rel_ref, o_ref, qs_ref, kt_ref, vx_ref, s_ref,
                bias_ref):
    nq = q_ref.shape[1] // TQ_CHK

    @pl.when(pl.program_id(0) == 0)
    def _():
        u = lax.broadcasted_iota(jnp.int32, (REL_PAD, ROLL_W), 1)
        r = lax.broadcasted_iota(jnp.int32, (REL_PAD, ROLL_W), 0)
        u = jnp.where(u < CHK_WIN, u, u - ROLL_W)
        idx = jnp.clip(CHK_BAND - u, -(CHUNK - 1), MAX_REL_DIST) + (CHUNK - 1)
        onehot = (r == idx).astype(BF16)
        g_all = _dot3(rel_ref[...], onehot)
        ri = lax.broadcasted_iota(jnp.int32, (TQ_CHK, CHK_WIN), 0) // CHUNK
        ci = lax.broadcasted_iota(jnp.int32, (TQ_CHK, CHK_WIN), 1) // CHUNK
        band = (ci >= ri) & (ci <= ri + CHUNK_LOOKBACK)
        for h in range(N_HEADS):
            g = jnp.broadcast_to(g_all[h:h + 1], (TQ_CHK, ROLL_W))
            toep = pltpu.roll(g, 0, 1, stride=1, stride_axis=0)
            bias_ref[h] = jnp.where(band, toep[:, :CHK_WIN], NEG_INF)

    _prep_streams(q_ref, k_ref, v_ref, qs_ref, kt_ref, vx_ref, v_masked=True)

    def bias_fn(grp, q0, st, k0, n):
        w0 = k0 - (q0 - CHK_BAND)
        return bias_ref[2 * grp + st, :, w0:w0 + n]

    def finish(t, r):
        out = r[0][:, :LANES] / r[0][:, LANES:] + r[1][:, :LANES] / r[1][:, LANES:]
        o_ref[0, t.q0:t.q0 + TQ_CHK, t.grp * LANES:(t.grp + 1) * LANES] = out.astype(BF16)

    tiles = [_QueryTile(grp, qb,
                        _key_blocks(max(0, qb * TQ_CHK - CHK_BAND), (qb + 1) * TQ_CHK, CHK_WIN),
                        functools.partial(bias_fn, grp, qb * TQ_CHK), TQ_CHK)
             for grp in range(CHK_W // LANES) for qb in range(nq)]
    _attend(tiles, qs_ref, kt_ref, vx_ref, s_ref, finish)


def _chk_attention(q, k, v, layer, rel_pad):
    b, s, _ = q.shape
    return pl.pallas_call(
        _chk_kernel,
        grid=(b,),
        in_specs=_qkv_specs(s, CHK_W) + [_layer_spec((8, REL_PAD), layer)],
        out_specs=pl.BlockSpec((1, s, CHK_W), lambda i: (i, 0, 0)),
        out_shape=jax.ShapeDtypeStruct((b, s, CHK_W), BF16),
        scratch_shapes=_attn_scratch(s, CHK_W, TQ_CHK, CHK_WIN) + [
            pltpu.VMEM((N_HEADS, TQ_CHK, CHK_WIN), F32)],
        compiler_params=_params(1),
        name="chunk_attn",
    )(q, k, v, rel_pad)


def _ffn_kernel(x_ref, fox_ref, diff_ref, chk_ref, wo_ref, g_ref, wup_ref, cw_ref, cb_ref,
                wdn_ref, gfin_ref, o_ref, carry_ref, act_ref, *, final_norm):
    tm = x_ref.shape[1]
    sub_rows = tm // FFN_SUBTILES

    @pl.when(pl.program_id(1) == 0)
    def _():
        carry_ref[...] = jnp.zeros_like(carry_ref)

    sub = lax.broadcasted_iota(jnp.int32, (8, FF_CHUNK), 0)

    def shifted(d, prev, n):
        r = pltpu.roll(d, n, 0)
        head = jnp.where(sub < n, pltpu.roll(prev, n, 0), r[:8])
        return jnp.concatenate([head, r[8:]], axis=0)

    def prologue(rows):
        mix = jnp.concatenate([fox_ref[0, rows], diff_ref[0, rows], chk_ref[0, rows]], axis=1)
        x1 = x_ref[0, rows] + _dot(mix, wo_ref[...])
        return x1, _rms(x1, g_ref[...]).astype(BF16)

    def up(rows, h):
        for c in range(D_FF // FF_CHUNK):
            ys = []
            for part in range(2):
                c0 = part * D_FF + c * FF_CHUNK
                cols = slice(c0, c0 + FF_CHUNK)
                d = _dot(h, wup_ref[:, cols])
                prev = carry_ref[:, cols]
                carry_ref[:, cols] = d[d.shape[0] - 8:]
                ys.append(cb_ref[:, cols]
                          + cw_ref[2:3, cols] * d
                          + cw_ref[1:2, cols] * shifted(d, prev, 1)
                          + cw_ref[0:1, cols] * shifted(d, prev, 2))
            hg = 0.5 * ys[1]
            act_ref[rows, c * FF_CHUNK:(c + 1) * FF_CHUNK] = (
                (hg + hg * jnp.tanh(hg)) * ys[0]).astype(BF16)

    def down(rows, x1):
        acc = x1 + _dot(act_ref[rows, :], wdn_ref[...])
        o_ref[0, rows] = _rms(acc, gfin_ref[...]) if final_norm else acc

    parts = [prologue(slice(i * sub_rows, (i + 1) * sub_rows)) for i in range(FFN_SUBTILES)]
    x1 = jnp.concatenate([p[0] for p in parts], axis=0)
    up(slice(0, tm), jnp.concatenate([p[1] for p in parts], axis=0))
    for i in range(FFN_SUBTILES):
        rows = slice(i * sub_rows, (i + 1) * sub_rows)
        down(rows, x1[rows])


def _out_ffn(x, fox_o, diff_o, chk_o, layer, w_out, g_ffn, w_up, conv_w, conv_b, w_down,
             g_final, final_norm):
    b, s, d = x.shape
    grid = (b, s // TM_FFN)
    row_spec = lambda n: pl.BlockSpec((1, TM_FFN, n), lambda i, j: (i, j, 0))
    single = lambda shape: _layer_spec(shape, layer, single_buffer=True)
    return pl.pallas_call(
        functools.partial(_ffn_kernel, final_norm=final_norm),
        grid=grid,
        in_specs=[row_spec(d), row_spec(FOX_W), row_spec(DIFF_W), row_spec(CHK_W),
                  single((FOX_W + DIFF_W + CHK_W, d)), _layer_spec((1, d), layer),
                  single((d, 2 * D_FF)),
                  _layer_spec((3, 2 * D_FF), layer), _layer_spec((1, 2 * D_FF), layer),
                  single((D_FF, d)), _const_spec((1, d))],
        out_specs=row_spec(d),
        out_shape=jax.ShapeDtypeStruct((b, s, d), F32),
        scratch_shapes=[pltpu.VMEM((8, 2 * D_FF), F32),
                        pltpu.VMEM((TM_FFN, D_FF), BF16)],
        compiler_params=_params(2),
        name="out_ffn",
    )(x, fox_o, diff_o, chk_o, w_out, g_ffn, w_up, conv_w, conv_b, w_down, g_final)


def kernel(x, g_mix, w_in, b_fox_f, diff_lambda, g_diff, rel_bias, w_out,
           g_ffn, w_ffn_in, conv_w, conv_b, w_ffn_out, g_final):
    depth = w_in.shape[0]
    d = x.shape[-1]
    o_ff = 3 * FOX_W
    o_rest = o_ff + N_HEADS
    slopes = jnp.asarray([2.0 ** (-8.0 * (i + 1) / N_HEADS_DIFF)
                          for i in range(N_HEADS_DIFF)], F32)
    slopes = jnp.broadcast_to(slopes[:, None, None], (N_HEADS_DIFF, 1, LANES))
    w_main = jnp.concatenate([w_in[:, :, :o_ff], w_in[:, :, o_rest:]], axis=2).astype(BF16)
    wff_t = jnp.zeros((depth, 8, d), BF16).at[:, :N_HEADS].set(
        jnp.swapaxes(w_in[:, :, o_ff:o_rest], 1, 2).astype(BF16))
    bff = jnp.zeros((depth, 8, 1), F32).at[:, :N_HEADS, 0].set(b_fox_f.astype(F32))
    linit = jnp.stack([jnp.full((1, LANES), 0.8 - 0.6 * math.exp(-0.3 * l), F32)
                       for l in range(depth)])
    rel_pad = jnp.zeros((depth, 8, REL_PAD), F32).at[:, :N_HEADS, :rel_bias.shape[2]].set(
        rel_bias.astype(F32))
    w_out_b, w_up_b, w_down_b = (w.astype(BF16) for w in (w_out, w_ffn_in, w_ffn_out))

    for l in range(depth):
        *qkv, cum = _in_proj(x, l, g_mix[:, None], w_main, wff_t, bff)
        fox_o = _fox_attention(*qkv[0:3], cum)
        diff_o = _diff_attention(*qkv[3:6], l, slopes, diff_lambda.astype(F32), linit,
                                 g_diff[:, None].astype(F32))
        chk_o = _chk_attention(*qkv[6:9], l, rel_pad)
        x = _out_ffn(x, fox_o, diff_o, chk_o, l, w_out_b, g_ffn[:, None], w_up_b, conv_w,
                     conv_b[:, None], w_down_b, g_final[None], final_norm=(l == depth - 1))
    return x
```
